```python
import functools
import jax
import jax.numpy as jnp
from jax import lax
import numpy as np

D_MODEL = 1024
BATCH = 4
SEQ = 8192
DEPTH = 1
DEC_BATCH = 128
DEC_SEQ = 8
PAST_LEN = 16384
PAGE_SIZE = 128

MLA_HEADS = 8
MLA_NOPE = 64
ROPE_DIM = 32
MLA_V = 64
Q_RANK = 256
KV_RANK = 128
RW_HEADS = 8
RW_HEAD_DIM = 64
RW_DIM = RW_HEADS * RW_HEAD_DIM
D_WLORA = 64
D_ALORA = 64
D_GLORA = 128
RW_COLS = 3 * RW_DIM + D_WLORA + D_ALORA + D_GLORA
D_MIX = MLA_HEADS * MLA_V + RW_DIM
IN_COLS = Q_RANK + KV_RANK + ROPE_DIM + RW_COLS
D_FF = 2816
ROPE_BASE = 10000.0
Q_BLOCK = 128
SM_SCALE = (MLA_NOPE + ROPE_DIM) ** -0.5
ALPHA = (2 * DEPTH) ** 0.25
BETA = (8 * DEPTH) ** -0.25
NEG = -1e30
LN_EPS = 1e-5
RMS_EPS = 1e-6
GN_EPS = 64e-5

kernel_name = 'hybrid_mla_rwkv7_macaron_deepnorm_step'


def _layernorm(x, g, b):
    xf = x.astype(jnp.float32)
    mu = jnp.mean(xf, -1, keepdims=True)
    var = jnp.mean(jnp.square(xf - mu), -1, keepdims=True)
    return ((xf - mu) * lax.rsqrt(var + LN_EPS) * g + b).astype(x.dtype)


def _rmsnorm(x, g):
    xf = x.astype(jnp.float32)
    return (xf * lax.rsqrt(jnp.mean(jnp.square(xf), -1, keepdims=True) + RMS_EPS) * g).astype(x.dtype)


def _swiglu(x, w_gate, w_up, w_down):
    return (jax.nn.silu(x @ w_gate) * (x @ w_up)) @ w_down


def _rope_tables(pos):
    inv = ROPE_BASE ** (-jnp.arange(0, ROPE_DIM, 2, dtype=jnp.float32) / ROPE_DIM)
    ang = pos[:, None] * inv[None, :]
    return jnp.cos(ang), jnp.sin(ang)


def _rope(x, cos, sin):
    xf = x.astype(jnp.float32)
    x1, x2 = jnp.split(xf, 2, axis=-1)
    return jnp.concatenate([x1 * cos - x2 * sin, x1 * sin + x2 * cos], axis=-1).astype(x.dtype)


def _mla_scores(q_lat, q_rope, ckv, krope):
    return (jnp.einsum('bqhr,bkr->bhqk', q_lat, ckv)
            + jnp.einsum('bqhp,bkp->bhqk', q_rope, krope)) * SM_SCALE


def _prompt_attend(q_lat, q_rope, ckv, krope):
    B, S, H, R = q_lat.shape
    qb = min(Q_BLOCK, S)
    nb = S // qb
    ckv_f = ckv.astype(jnp.float32)
    kr_f = krope.astype(jnp.float32)
    ql = q_lat.astype(jnp.float32).reshape(B, nb, qb, H, R).transpose(1, 0, 2, 3, 4)
    qr = q_rope.astype(jnp.float32).reshape(B, nb, qb, H, ROPE_DIM).transpose(1, 0, 2, 3, 4)
    kpos = jnp.arange(S)

    def block(args):
        i, q_i, qr_i = args
        qpos = i * qb + jnp.arange(qb)
        s = _mla_scores(q_i, qr_i, ckv_f, kr_f)
        s = jnp.where((qpos[:, None] >= kpos[None, :])[None, None], s, NEG)
        p = jax.nn.softmax(s, axis=-1)
        return jnp.einsum('bhqk,bkr->bqhr', p, ckv_f)

    o = lax.map(block, (jnp.arange(nb), ql, qr))
    return o.transpose(1, 0, 2, 3, 4).reshape(B, S, H, R).astype(q_lat.dtype)


def _online_update(carry, s, vals):
    m, l, acc = carry
    m_new = jnp.maximum(m, jnp.max(s, axis=-1))
    corr = jnp.exp(m - m_new)
    p = jnp.exp(s - m_new[..., None])
    l = l * corr + jnp.sum(p, axis=-1)
    acc = acc * corr[..., None] + jnp.einsum('bhqk,bkr->bhqr', p, vals)
    return (m_new, l, acc)


def _sample_attend(q_lat, q_rope, ckv_new, kr_new, cache_ckv, cache_krope, page_table):
    B, Q, H, R = q_lat.shape
    qf = q_lat.astype(jnp.float32)
    qr = q_rope.astype(jnp.float32)
    init = (jnp.full((B, H, Q), NEG, jnp.float32), jnp.zeros((B, H, Q), jnp.float32),
            jnp.zeros((B, H, Q, R), jnp.float32))

    def body(carry, pages):
        ckv_b = cache_ckv[pages].astype(jnp.float32)
        kr_b = cache_krope[pages].astype(jnp.float32)
        return _online_update(carry, _mla_scores(qf, qr, ckv_b, kr_b), ckv_b), None

    carry, _ = lax.scan(body, init, page_table.T)
    cn = ckv_new.astype(jnp.float32)
    s = _mla_scores(qf, qr, cn, kr_new.astype(jnp.float32))
    s = jnp.where(jnp.tril(jnp.ones((Q, Q), bool))[None, None], s, NEG)
    m, l, acc = _online_update(carry, s, cn)
    return (acc / l[..., None]).transpose(0, 2, 1, 3).astype(q_lat.dtype)


def _rwkv7_recurrence(state0, r, decay, k, v, a, b):
    def step(S, inp):
        r_t, w_t, k_t, v_t, a_t, b_t = inp
        sa = jnp.einsum('bhvk,bhk->bhv', S, a_t)
        S = S * w_t[:, :, None, :] + sa[..., None] * b_t[:, :, None, :] + v_t[..., None] * k_t[:, :, None, :]
        return S, jnp.einsum('bhvk,bhk->bhv', S, r_t)

    xs = tuple(jnp.moveaxis(t, 1, 0) for t in (r, decay, k, v, a, b))
    S, ys = lax.scan(step, state0, xs)
    return S, jnp.moveaxis(ys, 0, 1)


def _token_mixer(x, pos, shift_buf, wkv0, attend, w_in, q_norm_g, w_uq, kv_norm_g, w_uk, w_uv,
                 shift_mu, w0, w_lora_up, a0, a_lora_up, g_lora_up, k_k, k_a, r_k, lnx_g, lnx_b, w_out):
    B, S, _ = x.shape
    proj = x @ w_in
    c_q, c_kv, k_rope, rw = jnp.split(proj, [Q_RANK, Q_RANK + KV_RANK, Q_RANK + KV_RANK + ROPE_DIM], axis=-1)
    q = jnp.einsum('bsc,chd->bshd', _rmsnorm(c_q, q_norm_g), w_uq)
    q_nope, q_rope = q[..., :MLA_NOPE], q[..., MLA_NOPE:]
    ckv = _rmsnorm(c_kv, kv_norm_g)
    cos, sin = _rope_tables(pos)
    q_rope = _rope(q_rope, cos[:, None], sin[:, None])
    k_rope = _rope(k_rope, cos, sin)
    q_lat = jnp.einsum('bshn,rhn->bshr', q_nope, w_uk)
    o_lat = attend(q_lat, q_rope, ckv, k_rope)
    y_mla = jnp.einsum('bshr,rhv->bshv', o_lat, w_uv).reshape(B, S, MLA_HEADS * MLA_V)
    prev = jnp.concatenate([shift_buf[:, None].astype(rw.dtype), rw[:, :-1]], axis=1)
    rws = (rw + (prev - rw) * shift_mu).astype(jnp.float32)
    r, k, v, dw, da, dg = jnp.split(rws, [RW_DIM, 2 * RW_DIM, 3 * RW_DIM, 3 * RW_DIM + D_WLORA,
                                          3 * RW_DIM + D_WLORA + D_ALORA], axis=-1)
    w = -jax.nn.softplus(-(w0 + jnp.tanh(dw) @ w_lora_up)) - 0.5
    decay = jnp.exp(-jnp.exp(w))
    a = jax.nn.sigmoid(a0 + da @ a_lora_up)
    g = jax.nn.sigmoid(dg) @ g_lora_up

    def heads(t):
        return t.reshape(B, S, RW_HEADS, RW_HEAD_DIM)

    kk = heads(k * k_k)
    kk = kk / jnp.maximum(jnp.sqrt(jnp.sum(jnp.square(kk), -1, keepdims=True)), 1e-12)
    k = k * (1.0 + (a - 1.0) * k_a)
    r_h, k_h, v_h, a_h = heads(r), heads(k), heads(v), heads(a)
    wkv, o = _rwkv7_recurrence(wkv0.astype(jnp.float32), r_h, heads(decay), k_h, v_h, -kk, kk * a_h)
    mu = jnp.mean(o, -1, keepdims=True)
    var = jnp.mean(jnp.square(o - mu), -1, keepdims=True)
    o = (o - mu) * lax.rsqrt(var + GN_EPS) * lnx_g.reshape(RW_HEADS, RW_HEAD_DIM) + lnx_b.reshape(RW_HEADS, RW_HEAD_DIM)
    o = o + jnp.sum(r_h * k_h * r_k, axis=-1, keepdims=True) * v_h
    y_rw = (o.reshape(B, S, RW_DIM) * g).astype(x.dtype)
    y = jnp.concatenate([y_mla, y_rw], axis=-1) @ w_out
    return y, ckv, k_rope, wkv, rw[:, -1]


def _layer(x, pos, shift_buf, wkv0, attend, lp, mp):
    x = _layernorm(ALPHA * x + 0.5 * _swiglu(x, lp['ffa_w_gate'], lp['ffa_w_up'], lp['ffa_w_down']),
                   lp['ln1_g'], lp['ln1_b'])
    m, ckv, kr, wkv, buf = _token_mixer(x, pos, shift_buf, wkv0, attend, **mp)
    x = _layernorm(ALPHA * x + m, lp['ln2_g'], lp['ln2_b'])
    x = _layernorm(ALPHA * x + 0.5 * _swiglu(x, lp['ffb_w_gate'], lp['ffb_w_up'], lp['ffb_w_down']),
                   lp['ln3_g'], lp['ln3_b'])
    return x, ckv, kr, wkv, buf


def setup_inputs(seed: int = 0) -> dict:
    key = jax.random.key(seed)
    ks = iter(jax.random.split(key, 48))
    f32 = jnp.float32

    def nrm(shape, scale):
        return jax.random.normal(next(ks), shape, f32) * scale

    def gain(shape):
        return 1.0 + nrm(shape, 0.05)

    def bias(shape):
        return nrm(shape, 0.02)

    L = DEPTH
    n_pages = PAST_LEN // PAGE_SIZE
    n_used = DEC_BATCH * n_pages
    n_phys = n_used + n_used // 4
    page_table = jax.random.permutation(next(ks), n_phys)[:n_used].reshape(DEC_BATCH, n_pages).astype(jnp.int32)
    return {
        'x_prompt': nrm((BATCH, SEQ, D_MODEL), 1.0),
        'x_sample': nrm((DEC_BATCH, DEC_SEQ, D_MODEL), 1.0),
        'cache_ckv': nrm((L, n_phys, PAGE_SIZE, KV_RANK), 1.0),
        'cache_krope': nrm((L, n_phys, PAGE_SIZE, ROPE_DIM), 1.0),
        'state_wkv': nrm((L, DEC_BATCH, RW_HEADS, RW_HEAD_DIM, RW_HEAD_DIM), 0.3),
        'state_shift': nrm((L, DEC_BATCH, RW_COLS), 1.0),
        'page_table': page_table,
        'ln1_g': gain((L, D_MODEL)),
        'ln1_b': bias((L, D_MODEL)),
        'ffa_w_gate': nrm((L, D_MODEL, D_FF), D_MODEL ** -0.5),
        'ffa_w_up': nrm((L, D_MODEL, D_FF), D_MODEL ** -0.5),
        'ffa_w_down': nrm((L, D_FF, D_MODEL), BETA * D_FF ** -0.5),
        'w_in': nrm((L, D_MODEL, IN_COLS), D_MODEL ** -0.5),
        'q_norm_g': gain((L, Q_RANK)),
        'w_uq': nrm((L, Q_RANK, MLA_HEADS, MLA_NOPE + ROPE_DIM), Q_RANK ** -0.5),
        'kv_norm_g': gain((L, KV_RANK)),
        'w_uk': nrm((L, KV_RANK, MLA_HEADS, MLA_NOPE), KV_RANK ** -0.5),
        'w_uv': nrm((L, KV_RANK, MLA_HEADS, MLA_V), KV_RANK ** -0.5),
        'shift_mu': jax.random.uniform(next(ks), (L, RW_COLS), f32),
        'w0': -1.0 + nrm((L, RW_DIM), 0.5),
        'w_lora_up': nrm((L, D_WLORA, RW_DIM), D_WLORA ** -0.5),
        'a0': nrm((L, RW_DIM), 0.1),
        'a_lora_up': nrm((L, D_ALORA, RW_DIM), D_ALORA ** -0.5),
        'g_lora_up': nrm((L, D_GLORA, RW_DIM), D_GLORA ** -0.5),
        'k_k': 0.85 + nrm((L, RW_DIM), 0.05),
        'k_a': 1.0 + nrm((L, RW_DIM), 0.05),
        'r_k': nrm((L, RW_HEADS, RW_HEAD_DIM), 0.1),
        'lnx_g': gain((L, RW_DIM)),
        'lnx_b': bias((L, RW_DIM)),
        'w_out': nrm((L, D_MIX, D_MODEL), BETA * D_MIX ** -0.5),
        'ln2_g': gain((L, D_MODEL)),
        'ln2_b': bias((L, D_MODEL)),
        'ffb_w_gate': nrm((L, D_MODEL, D_FF), D_MODEL ** -0.5),
        'ffb_w_up': nrm((L, D_MODEL, D_FF), D_MODEL ** -0.5),
        'ffb_w_down': nrm((L, D_FF, D_MODEL), BETA * D_FF ** -0.5),
        'ln3_g': gain((L, D_MODEL)),
        'ln3_b': bias((L, D_MODEL)),
    }


def reference(x_prompt, x_sample, cache_ckv, cache_krope, state_wkv, state_shift, page_table,
              ln1_g, ln1_b, ffa_w_gate, ffa_w_up, ffa_w_down, w_in, q_norm_g, w_uq, kv_norm_g, w_uk, w_uv,
              shift_mu, w0, w_lora_up, a0, a_lora_up, g_lora_up, k_k, k_a, r_k, lnx_g, lnx_b, w_out,
              ln2_g, ln2_b, ffb_w_gate, ffb_w_up, ffb_w_down, ln3_g, ln3_b):
    past_len = page_table.shape[1] * cache_ckv.shape[2]
    pos_p = jnp.arange(x_prompt.shape[1], dtype=jnp.float32)
    pos_s = jnp.arange(x_sample.shape[1], dtype=jnp.float32) + float(past_len)
    bp = x_prompt.shape[0]
    xp, xs = x_prompt, x_sample
    ckv_p, kr_p, wkv_p, sh_p = [], [], [], []
    ckv_s, kr_s, wkv_s, sh_s = [], [], [], []
    for l in range(DEPTH):
        lp = dict(ln1_g=ln1_g[l], ln1_b=ln1_b[l], ffa_w_gate=ffa_w_gate[l], ffa_w_up=ffa_w_up[l],
                  ffa_w_down=ffa_w_down[l], ln2_g=ln2_g[l], ln2_b=ln2_b[l], ffb_w_gate=ffb_w_gate[l],
                  ffb_w_up=ffb_w_up[l], ffb_w_down=ffb_w_down[l], ln3_g=ln3_g[l], ln3_b=ln3_b[l])
        mp = dict(w_in=w_in[l], q_norm_g=q_norm_g[l], w_uq=w_uq[l], kv_norm_g=kv_norm_g[l], w_uk=w_uk[l],
                  w_uv=w_uv[l], shift_mu=shift_mu[l], w0=w0[l], w_lora_up=w_lora_up[l], a0=a0[l],
                  a_lora_up=a_lora_up[l], g_lora_up=g_lora_up[l], k_k=k_k[l], k_a=k_a[l], r_k=r_k[l],
                  lnx_g=lnx_g[l], lnx_b=lnx_b[l], w_out=w_out[l])
        buf0 = jnp.zeros((bp, RW_COLS), xp.dtype)
        wkv0 = jnp.zeros((bp, RW_HEADS, RW_HEAD_DIM, RW_HEAD_DIM), jnp.float32)
        xp, c1, k1, s1, b1 = _layer(xp, pos_p, buf0, wkv0, _prompt_attend, lp, mp)
        attend_s = functools.partial(_sample_attend, cache_ckv=cache_ckv[l], cache_krope=cache_krope[l],
                                     page_table=page_table)
        xs, c2, k2, s2, b2 = _layer(xs, pos_s, state_shift[l], state_wkv[l], attend_s, lp, mp)
        ckv_p.append(c1); kr_p.append(k1); wkv_p.append(s1); sh_p.append(b1)
        ckv_s.append(c2); kr_s.append(k2); wkv_s.append(s2); sh_s.append(b2)
    return (xp, xs, jnp.stack(ckv_p), jnp.stack(kr_p), jnp.stack(wkv_p), jnp.stack(sh_p),
            jnp.stack(ckv_s), jnp.stack(kr_s), jnp.stack(wkv_s), jnp.stack(sh_s))
```

```python
import functools
import math

import jax
import jax.numpy as jnp
from jax import lax
from jax.experimental import pallas as pl
from jax.experimental.pallas import tpu as pltpu

F32 = jnp.float32
BF16 = jnp.bfloat16

LANES = 128
HEAD_PAIR = 2
LN_EPS = 1e-5
RMS_EPS = 1e-6
GN_EPS = 64e-5
ROPE_BASE = 10000.0
NEG = -1e30
VMEM_LIMIT = 56 * 1024 * 1024


def _dot(a, b):
    return jnp.dot(a, b, preferred_element_type=F32)


def _dot_nt(a, b):
    return lax.dot_general(a, b, (((1,), (1,)), ((), ())), preferred_element_type=F32)


def _dot_tn(a, b):
    return lax.dot_general(a, b, (((0,), (0,)), ((), ())), preferred_element_type=F32)


def _layernorm(y, g, b):
    mu = jnp.mean(y, axis=-1, keepdims=True)
    d = y - mu
    var = jnp.mean(d * d, axis=-1, keepdims=True)
    return d * lax.rsqrt(var + LN_EPS) * g + b


def _rmsnorm(y, g):
    return y * lax.rsqrt(jnp.mean(y * y, axis=-1, keepdims=True) + RMS_EPS) * g


def _sigmoid(x):
    return 1.0 / (1.0 + jnp.exp(-x))


def _split_dot(x, w_bf16):
    hi = x.astype(BF16)
    lo = (x - hi.astype(F32)).astype(BF16)
    return _dot(hi, w_bf16) + _dot(lo, w_bf16)


def _const_spec(shape):
    nd = len(shape)
    return pl.BlockSpec(shape, lambda *_: (0,) * nd, pipeline_mode=pl.Buffered(1))


def _swiglu_ln(x, wg_ref, wu_ref, wd_ref, g_ref, b_ref, alpha, ff_chunk):
    xb = x.astype(BF16)
    d_ff = wg_ref.shape[1]
    acc = jnp.zeros(x.shape, F32)
    for c in range(d_ff // ff_chunk):
        sl = slice(c * ff_chunk, (c + 1) * ff_chunk)
        gate = _dot(xb, wg_ref[:, sl])
        up = _dot(xb, wu_ref[:, sl])
        h = (gate * _sigmoid(gate) * up).astype(BF16)
        acc = acc + _dot(h, wd_ref[sl, :])
    return _layernorm(alpha * x + 0.5 * acc, g_ref[...], b_ref[...])


def _ffn_ln_kernel(x_ref, wg_ref, wu_ref, wd_ref, g_ref, b_ref, o_ref, *, alpha, ff_chunk):
    o_ref[...] = _swiglu_ln(x_ref[...], wg_ref, wu_ref, wd_ref, g_ref, b_ref, alpha, ff_chunk)


def _out_ffn_ln_kernel(x_ref, ocat_ref, yrw_ref, wuv_ref, woa_ref, wob_ref, g2_ref, b2_ref,
                       wg_ref, wu_ref, wd_ref, g3_ref, b3_ref, o_ref, *, alpha, ff_chunk):
    y_mla = _dot(ocat_ref[...], wuv_ref[...]).astype(BF16)
    y = _dot(y_mla, woa_ref[...]) + _dot(yrw_ref[...], wob_ref[...])
    x2 = _layernorm(alpha * x_ref[...] + y, g2_ref[...], b2_ref[...])
    o_ref[...] = _swiglu_ln(x2, wg_ref, wu_ref, wd_ref, g3_ref, b3_ref, alpha, ff_chunk)


def _row_tile(n, want):
    t = min(n, want)
    assert n % t == 0
    return t


def _ffn_chunk(d_ff):
    for c in (512, 256, 128):
        if d_ff % c == 0:
            return c
    return d_ff


def ffn_ln(x, wg, wu, wd, g, b, *, alpha, tm=512):
    n, d = x.shape
    tm = _row_tile(n, tm)
    kern = functools.partial(_ffn_ln_kernel, alpha=alpha, ff_chunk=_ffn_chunk(wg.shape[1]))
    row = pl.BlockSpec((tm, d), lambda i: (i, 0))
    return pl.pallas_call(
        kern, grid=(n // tm,),
        in_specs=[row, _const_spec(wg.shape), _const_spec(wu.shape), _const_spec(wd.shape),
                  _const_spec(g.shape), _const_spec(b.shape)],
        out_specs=row, out_shape=jax.ShapeDtypeStruct((n, d), F32),
        compiler_params=pltpu.CompilerParams(dimension_semantics=("arbitrary",), vmem_limit_bytes=VMEM_LIMIT),
        name="ffn_ln",
    )(x, wg, wu, wd, g, b)


def out_ffn_ln(x, ocat, yrw, wuv, woa, wob, g2, b2, wg, wu, wd, g3, b3, *, alpha, tm=512):
    n, d = x.shape
    tm = _row_tile(n, tm)
    kern = functools.partial(_out_ffn_ln_kernel, alpha=alpha, ff_chunk=_ffn_chunk(wg.shape[1]))
    row = lambda w: pl.BlockSpec((tm, w), lambda i: (i, 0))
    consts = (wuv, woa, wob, g2, b2, wg, wu, wd, g3, b3)
    return pl.pallas_call(
        kern, grid=(n // tm,),
        in_specs=[row(d), row(ocat.shape[1]), row(yrw.shape[1])] + [_const_spec(c.shape) for c in consts],
        out_specs=row(d), out_shape=jax.ShapeDtypeStruct((n, d), F32),
        compiler_params=pltpu.CompilerParams(dimension_semantics=("arbitrary",), vmem_limit_bytes=VMEM_LIMIT),
        name="out_ffn_ln",
    )(x, ocat, yrw, *consts)


def _mixer_in_kernel(x_ref, sb_ref, wq_ref, wkv_ref, wkr_ref, wrw_ref, qg_ref, kvg_ref, wuqn_ref, wuqr_ref,
                     wuk_ref, invf_ref, mu_ref, w0_ref, wl_ref, a0_ref, al_ref, gl_ref, kk_ref, ka_ref,
                     hones_ref,
                     q_out, kvcat_out, ckv_out, kr_out, r_out, lw_out, k_out, v_out, kkn_out, as_out, g_out,
                     sh_out, rw_sc, *, seq, tm, pos0, n_heads, sm_scale, rw_dim, rope_dim):
    t = pl.program_id(1)
    xb = x_ref[...].astype(BF16)
    whole_seqs = tm >= seq

    row = lax.broadcasted_iota(jnp.int32, (tm, 1), 0)
    row_in_seq = (row % seq) if whole_seqs else (row + t * tm)
    pos = row_in_seq.astype(F32) + pos0
    ang = pos * invf_ref[...]
    lane = lax.broadcasted_iota(jnp.int32, (tm, LANES), 1)
    cosm = jnp.where(lane < rope_dim, jnp.cos(ang), 0.0)
    sinm = jnp.where(lane < rope_dim, jnp.sin(ang), 0.0)

    def rope(grp):
        return grp * cosm + pltpu.roll(grp, LANES - rope_dim, axis=1) * sinm

    ckv = _rmsnorm(_dot(xb, wkv_ref[...]), kvg_ref[...])
    kr = rope(_dot(xb, wkr_ref[...]))
    ckv_out[...] = ckv
    kr_out[...] = kr[:, :rope_dim]
    kvcat_out[:, :LANES] = ckv.astype(BF16)
    kvcat_out[:, LANES:] = kr.astype(BF16)

    cq = _rmsnorm(_dot(xb, wq_ref[...]), qg_ref[...]).astype(BF16)
    q_lat = _dot(_dot(cq, wuqn_ref[...]).astype(BF16), wuk_ref[...])
    q_rg = _dot(cq, wuqr_ref[...])
    nbg, rows = q_out.shape[0], q_out.shape[2]
    for h in range(n_heads):
        sl = slice(h * LANES, (h + 1) * LANES)
        q_h = jnp.concatenate([q_lat[:, sl] * sm_scale, rope(q_rg[:, sl]) * sm_scale], axis=-1).astype(BF16)
        q_out[:, h] = q_h.reshape(nbg, rows, 2 * LANES)

    rw = _dot(xb, wrw_ref[...])
    rw_sc[8:tm + 8, :] = rw
    if whole_seqs:
        nb = tm // seq
        rw_sc[7:8, :] = jnp.zeros((1, rw.shape[1]), F32)
        prev = rw_sc[7:tm + 7, :]
        first = jnp.broadcast_to(sb_ref[...], (nb, seq, rw.shape[1])).reshape(tm, rw.shape[1])
        prev = jnp.where(row % seq == 0, first, prev)
        sh_out[...] = rw.reshape(nb, seq, rw.shape[1])[:, seq - 1:seq, :]
    else:
        @pl.when(t == 0)
        def _():
            rw_sc[7:8, :] = sb_ref[0]
        prev = rw_sc[7:tm + 7, :]
        rw_sc[7:8, :] = rw[tm - 1:tm, :]
        sh_out[0] = rw[tm - 1:tm, :]
    rws = rw + (prev - rw) * mu_ref[...]

    r = rws[:, :rw_dim]
    k = rws[:, rw_dim:2 * rw_dim]
    v = rws[:, 2 * rw_dim:3 * rw_dim]
    lora_in = rws[:, 3 * rw_dim:3 * rw_dim + LANES]
    dg = rws[:, 3 * rw_dim + LANES:]
    z = w0_ref[...] + _dot(jnp.tanh(lora_in).astype(BF16), wl_ref[...])
    nz = -z
    softplus = jnp.maximum(nz, 0.0) + jnp.log(1.0 + jnp.exp(-jnp.abs(nz)))
    w = -softplus - 0.5
    a = _sigmoid(a0_ref[...] + _dot(lora_in.astype(BF16), al_ref[...]))
    g = _dot(_sigmoid(dg).astype(BF16), gl_ref[...])
    kk = k * kk_ref[...]
    norm = jnp.sqrt(_split_dot(kk * kk, hones_ref[...]))
    kk = kk / jnp.maximum(norm, 1e-12)
    r_out[...] = r
    lw_out[...] = -jnp.exp(w)
    k_out[...] = k * (1.0 + (a - 1.0) * ka_ref[...])
    v_out[...] = v
    kkn_out[...] = kk
    as_out[...] = a
    g_out[...] = g


def mixer_in(x, shift_buf, wts, *, n_seq, seq, pos0, tm, n_heads, sm_scale, rw_dim, rope_dim):
    n, d = x.shape
    rw_cols = wts["wrw"].shape[1]
    if tm >= seq:
        assert tm % seq == 0 and n % tm == 0
        nb = tm // seq
        grid = (n // tm, 1)
        rows = seq
        nbg = nb
        row_idx = lambda i, t: (i, 0)
        sb_spec = pl.BlockSpec((nb, 1, rw_cols), lambda i, t: (i, 0, 0))
        q_spec = pl.BlockSpec((nb, n_heads, seq, 2 * LANES), lambda i, t: (i, 0, 0, 0))
        q_shape = (n_seq, n_heads, seq, 2 * LANES)
    else:
        assert seq % tm == 0
        tps = seq // tm
        grid = (n_seq, tps)
        rows = tm
        nbg = 1
        row_idx = lambda i, t: (i * tps + t, 0)
        sb_spec = pl.BlockSpec((1, 1, rw_cols), lambda i, t: (i, 0, 0))
        q_spec = pl.BlockSpec((1, n_heads, tm, 2 * LANES), lambda i, t: (i * tps + t, 0, 0, 0))
        q_shape = (n_seq * tps, n_heads, tm, 2 * LANES)
    rowspec = lambda w: pl.BlockSpec((tm, w), row_idx)
    names = ("wq", "wkv", "wkr", "wrw", "qg", "kvg", "wuqn", "wuqr", "wuk", "invf", "mu", "w0", "wl", "a0",
             "al", "gl", "kk", "ka", "hones")
    consts = [wts[k] for k in names]
    kern = functools.partial(_mixer_in_kernel, seq=seq, tm=tm, pos0=float(pos0), n_heads=n_heads,
                             sm_scale=sm_scale, rw_dim=rw_dim, rope_dim=rope_dim)
    f = lambda w, dt=F32: jax.ShapeDtypeStruct((n, w), dt)
    out_shape = (jax.ShapeDtypeStruct(q_shape, BF16), f(2 * LANES, BF16), f(LANES), f(rope_dim)) \
        + (f(rw_dim),) * 7 + (jax.ShapeDtypeStruct((n_seq, 1, rw_cols), F32),)
    out_specs = (q_spec, rowspec(2 * LANES), rowspec(LANES), rowspec(rope_dim)) + (rowspec(rw_dim),) * 7 + (sb_spec,)
    return pl.pallas_call(
        kern, grid=grid,
        in_specs=[rowspec(d), sb_spec] + [_const_spec(c.shape) for c in consts],
        out_specs=out_specs, out_shape=out_shape,
        scratch_shapes=[pltpu.VMEM((tm + 8, rw_cols), F32)],
        compiler_params=pltpu.CompilerParams(dimension_semantics=("arbitrary", "arbitrary"),
                                             vmem_limit_bytes=VMEM_LIMIT),
        name="mixer_in",
    )(x, shift_buf, *consts)


def _softmax_update(s, vals, m_sc, l_sc, acc_sc):
    m_prev = m_sc[...]
    m_new = jnp.maximum(m_prev, jnp.max(s, axis=-1, keepdims=True))
    corr = jnp.exp(m_prev - m_new)
    p = jnp.exp(s - m_new)
    l_sc[...] = l_sc[...] * corr + jnp.sum(p, axis=-1, keepdims=True)
    acc_sc[...] = acc_sc[...] * corr + _dot(p.astype(BF16), vals)
    m_sc[...] = m_new


def _softmax_init(m_sc, l_sc, acc_sc):
    m_sc[...] = jnp.full(m_sc.shape, NEG, F32)
    l_sc[...] = jnp.zeros(l_sc.shape, F32)
    acc_sc[...] = jnp.zeros(acc_sc.shape, F32)


def _prompt_attn_kernel(q_ref, kv_ref, o_ref, m_sc, l_sc, acc_sc, *, tq, n_heads):
    qb = pl.program_id(1)
    q = q_ref[0]
    _softmax_init(m_sc, l_sc, acc_sc)

    def block(kb, masked):
        k = kv_ref[0, pl.ds(pl.multiple_of(kb * tq, tq), tq), :]
        s = _dot_nt(q, k)
        if masked:
            qi = lax.broadcasted_iota(jnp.int32, s.shape, 0) % tq
            kj = lax.broadcasted_iota(jnp.int32, s.shape, 1)
            s = jnp.where(qi >= kj, s, NEG)
        _softmax_update(s, k[:, :LANES], m_sc, l_sc, acc_sc)

    def body(kb, carry):
        block(kb, False)
        return carry

    lax.fori_loop(0, qb, body, 0)
    block(qb, True)
    o = (acc_sc[...] / l_sc[...]).astype(BF16)
    for h in range(n_heads):
        o_ref[0, :, h * LANES:(h + 1) * LANES] = o[h * tq:(h + 1) * tq]


def prompt_attn(q, kvcat, *, n_seq, seq, tq, n_heads):
    nq = seq // tq
    m = n_heads * tq
    kern = functools.partial(_prompt_attn_kernel, tq=tq, n_heads=n_heads)
    return pl.pallas_call(
        kern, grid=(n_seq, nq),
        in_specs=[pl.BlockSpec((1, m, 2 * LANES), lambda b, i: (b * nq + i, 0, 0)),
                  pl.BlockSpec((1, seq, 2 * LANES), lambda b, i: (b, 0, 0))],
        out_specs=pl.BlockSpec((1, tq, n_heads * LANES), lambda b, i: (b, i, 0)),
        out_shape=jax.ShapeDtypeStruct((n_seq, seq, n_heads * LANES), BF16),
        scratch_shapes=[pltpu.VMEM((m, 1), F32), pltpu.VMEM((m, 1), F32), pltpu.VMEM((m, LANES), F32)],
        compiler_params=pltpu.CompilerParams(dimension_semantics=("arbitrary", "arbitrary"),
                                             vmem_limit_bytes=VMEM_LIMIT),
        name="prompt_attn",
    )(q, kvcat)


def _sample_attn_kernel(pt_ref, q_ref, kvnew_ref, *refs, pages_per_step, page, rope_dim, dec_seq, n_heads):
    ckv_refs = refs[:pages_per_step]
    kr_refs = refs[pages_per_step:2 * pages_per_step]
    o_ref, m_sc, l_sc, acc_sc = refs[2 * pages_per_step:]
    g = pl.program_id(1)
    q = q_ref[0]

    @pl.when(g == 0)
    def _():
        _softmax_init(m_sc, l_sc, acc_sc)

    per_row = LANES // rope_dim
    view_rows = page // per_row
    lane = lax.broadcasted_iota(jnp.int32, (view_rows, LANES), 1)
    c_parts, kr_parts = [], []
    for ckv_ref, kr_ref in zip(ckv_refs, kr_refs):
        x = kr_ref[...]
        for u in range(per_row):
            c_parts.append(ckv_ref[pl.ds(u, view_rows, stride=per_row), :].astype(BF16))
            xu = x if u == 0 else pltpu.roll(x, LANES - u * rope_dim, axis=1)
            kr_parts.append(jnp.where(lane < rope_dim, xu, 0.0).astype(BF16))
    c_all = jnp.concatenate(c_parts, axis=0)
    s = _dot_nt(q[:, :LANES], c_all) + _dot_nt(q[:, LANES:], jnp.concatenate(kr_parts, axis=0))
    _softmax_update(s, c_all, m_sc, l_sc, acc_sc)

    @pl.when(g == pl.num_programs(1) - 1)
    def _():
        kn = kvnew_ref[0]
        s = _dot_nt(q, kn)
        qi = lax.broadcasted_iota(jnp.int32, s.shape, 0) % dec_seq
        kj = lax.broadcasted_iota(jnp.int32, s.shape, 1)
        s = jnp.where(qi >= kj, s, NEG)
        _softmax_update(s, kn[:, :LANES], m_sc, l_sc, acc_sc)
        o = (acc_sc[...] / l_sc[...]).astype(BF16)
        for h in range(n_heads):
            o_ref[0, :, h * LANES:(h + 1) * LANES] = o[h * dec_seq:(h + 1) * dec_seq]


def sample_attn(q, kvnew, cache_ckv, cache_kr_view, page_table, *, dec_seq, n_heads, rope_dim, pages_per_step=8):
    b, n_pages = page_table.shape
    page = cache_ckv.shape[1]
    view_rows = cache_kr_view.shape[1]
    pps = pages_per_step
    assert n_pages % pps == 0
    m = n_heads * dec_seq
    kern = functools.partial(_sample_attn_kernel, pages_per_step=pps, page=page, rope_dim=rope_dim,
                             dec_seq=dec_seq, n_heads=n_heads)

    def page_spec(rows, j):
        return pl.BlockSpec((None, rows, LANES), lambda i, g, pt: (pt[i * n_pages + g * pps + j], 0, 0))

    grid_spec = pltpu.PrefetchScalarGridSpec(
        num_scalar_prefetch=1, grid=(b, n_pages // pps),
        in_specs=[pl.BlockSpec((1, m, 2 * LANES), lambda i, g, pt: (i, 0, 0)),
                  pl.BlockSpec((1, dec_seq, 2 * LANES), lambda i, g, pt: (i, 0, 0))]
        + [page_spec(page, j) for j in range(pps)] + [page_spec(view_rows, j) for j in range(pps)],
        out_specs=pl.BlockSpec((1, dec_seq, n_heads * LANES), lambda i, g, pt: (i, 0, 0)),
        scratch_shapes=[pltpu.VMEM((m, 1), F32), pltpu.VMEM((m, 1), F32), pltpu.VMEM((m, LANES), F32)])
    return pl.pallas_call(
        kern, grid_spec=grid_spec,
        out_shape=jax.ShapeDtypeStruct((b, dec_seq, n_heads * LANES), BF16),
        compiler_params=pltpu.CompilerParams(dimension_semantics=("arbitrary", "arbitrary"),
                                             vmem_limit_bytes=VMEM_LIMIT),
        name="sample_attn",
    )(page_table.reshape(-1), q, kvnew, *([cache_ckv] * pps), *([cache_kr_view] * pps))


def _rwkv_pair_chunk(r, lw, k, v, kk, asig, s0, tri_incl, strict, incl, head0):
    c = r.shape[0]
    cs = jnp.dot(tri_incl, lw, preferred_element_type=F32, precision=lax.Precision.HIGHEST)
    cs_end = cs[c - 1:c, :]
    e_neg = jnp.exp(-cs)
    e_end = jnp.exp(cs_end - cs)
    bvec = kk * asig

    def stack(x):
        return jnp.concatenate([jnp.where(head0, x, 0.0), jnp.where(head0, 0.0, x)], axis=0)

    a_t = stack(-kk * jnp.exp(cs - lw))
    r_t = stack(r * jnp.exp(cs))
    k_t = stack(k * e_neg)
    b_t = stack(bvec * e_neg)
    k_e = stack(k * e_end)
    b_e = stack(bvec * e_end)
    v_s = stack(v).astype(BF16)

    c2 = 2 * c
    sc = _dot_nt(jnp.concatenate([a_t, r_t], axis=0).astype(BF16), jnp.concatenate([k_t, b_t], axis=0).astype(BF16))
    l_ak = jnp.where(strict, sc[:c2, :c2], 0.0).astype(BF16)
    l_ab = jnp.where(strict, sc[:c2, c2:], 0.0)
    a_rk = jnp.where(incl, sc[c2:, :c2], 0.0).astype(BF16)
    a_rb = jnp.where(incl, sc[c2:, c2:], 0.0).astype(BF16)

    x = jnp.concatenate([a_t, _dot(l_ak, v_s)], axis=-1)
    lp = l_ab
    n_fac = int(math.log2(c))
    for i in range(n_fac):
        lpb = lp.astype(BF16)
        x = x + _dot(lpb, x.astype(BF16))
        if i + 1 < n_fac:
            lp = _dot(lpb, lpb)
    a_hat = x[:, :LANES]
    w_v = x[:, LANES:]

    s0b = s0.astype(BF16)
    u = (_dot_nt(a_hat.astype(BF16), s0b) + w_v).astype(BF16)
    y_st = _dot_nt(r_t.astype(BF16), s0b) + _dot(a_rk, v_s) + _dot(a_rb, u)
    y = y_st[:c] + y_st[c:]
    s_new = s0 * jnp.exp(cs_end) + _dot_tn(v_s, k_e.astype(BF16)) + _dot_tn(u, b_e.astype(BF16))
    return y, s_new


def _rwkv_kernel(r_ref, lw_ref, k_ref, v_ref, kk_ref, as_ref, g_ref, s0_ref, rk_ref, lg_ref, lb_ref, pones_ref,
                 y_ref, sout_ref, s_sc, *, chunk, n_pairs):
    ci = pl.program_id(1)

    @pl.when(ci == 0)
    def _():
        s_sc[...] = s0_ref[0]

    c = chunk
    ti = lax.broadcasted_iota(jnp.int32, (c, c), 0)
    tj = lax.broadcasted_iota(jnp.int32, (c, c), 1)
    tri_incl = (ti >= tj).astype(F32)
    si = lax.broadcasted_iota(jnp.int32, (2 * c, 2 * c), 0)
    sj = lax.broadcasted_iota(jnp.int32, (2 * c, 2 * c), 1)
    strict = si > sj
    incl = si >= sj
    head0 = lax.broadcasted_iota(jnp.int32, (c, LANES), 1) < (LANES // HEAD_PAIR)
    pones = pones_ref[...]
    inv_n = 1.0 / (LANES // HEAD_PAIR)

    for p in range(n_pairs):
        sl = slice(p * LANES, (p + 1) * LANES)
        r, k, v = r_ref[:, sl], k_ref[:, sl], v_ref[:, sl]
        y, s_new = _rwkv_pair_chunk(r, lw_ref[:, sl], k, v, kk_ref[:, sl], as_ref[:, sl], s_sc[p],
                                    tri_incl, strict, incl, head0)
        s_sc[p] = s_new
        mu = _split_dot(y, pones) * inv_n
        d = y - mu
        var = _split_dot(d * d, pones) * inv_n
        o = d * lax.rsqrt(var + GN_EPS) * lg_ref[:, sl] + lb_ref[:, sl]
        o = o + _split_dot(r * k * rk_ref[:, sl], pones) * v
        y_ref[:, sl] = (o * g_ref[:, sl]).astype(BF16)

    @pl.when(ci == pl.num_programs(1) - 1)
    def _():
        sout_ref[0] = s_sc[...]


def rwkv_chunked(r, lw, k, v, kk, asig, g, s0, rk, lg, lb, pones, *, n_seq, seq, chunk):
    n, rw_dim = r.shape
    n_pairs = rw_dim // LANES
    nc = seq // chunk
    kern = functools.partial(_rwkv_kernel, chunk=chunk, n_pairs=n_pairs)
    row = pl.BlockSpec((chunk, rw_dim), lambda b, c: (b * nc + c, 0))
    st = pl.BlockSpec((1, n_pairs, LANES, LANES), lambda b, c: (b, 0, 0, 0))
    return pl.pallas_call(
        kern, grid=(n_seq, nc),
        in_specs=[row] * 7 + [st] + [_const_spec(x.shape) for x in (rk, lg, lb, pones)],
        out_specs=(row, st),
        out_shape=(jax.ShapeDtypeStruct((n, rw_dim), BF16), jax.ShapeDtypeStruct(s0.shape, F32)),
        scratch_shapes=[pltpu.VMEM((n_pairs, LANES, LANES), F32)],
        compiler_params=pltpu.CompilerParams(dimension_semantics=("arbitrary", "arbitrary"),
                                             vmem_limit_bytes=VMEM_LIMIT),
        name="rwkv_chunked",
    )(r, lw, k, v, kk, asig, g, s0, rk, lg, lb, pones)


def _block_diag(blocks):
    n, r, c = blocks.shape
    eye = jnp.eye(n, dtype=blocks.dtype)
    return (eye[:, None, :, None] * blocks[:, :, None, :]).reshape(n * r, n * c)


def _rope_group(w, rope_dim):
    half = rope_dim // 2
    x1, x2 = w[..., :half], w[..., half:]
    pad = jnp.zeros(w.shape[:-1] + (LANES - 2 * rope_dim,), w.dtype)
    return jnp.concatenate([x1, x2, -x2, x1, pad], axis=-1)


def _prep_weights(p, dims):
    q_rank, kv_rank, rope_dim, n_heads, nope, rw_dim = (dims[k] for k in
                                                        ("q_rank", "kv_rank", "rope_dim", "n_heads", "nope", "rw_dim"))
    w_in = p["w_in"]
    o1, o2 = q_rank + kv_rank, q_rank + kv_rank + rope_dim
    w_uq = p["w_uq"]
    row = lambda a: a.reshape(1, -1).astype(F32)
    d_wl = p["w_lora_up"].shape[0]
    d_al = p["a_lora_up"].shape[0]
    assert d_wl + d_al == LANES
    half = jnp.arange(0, rope_dim, 2, dtype=F32)
    inv = ROPE_BASE ** (-half / rope_dim)
    invf = jnp.concatenate([inv, inv, jnp.zeros((LANES - rope_dim,), F32)]).reshape(1, LANES)
    rw_head = rw_dim // dims["rw_heads"]
    ones_blk = lambda width: _block_diag(jnp.ones((width // rw_head, rw_head, rw_head), F32)).astype(BF16)
    w_out = p["w_out"]
    d_mla = n_heads * dims["mla_v"]
    return {
        "wq": w_in[:, :q_rank].astype(BF16),
        "wkv": w_in[:, q_rank:o1].astype(BF16),
        "wkr": _rope_group(w_in[:, o1:o2], rope_dim).astype(BF16),
        "wrw": w_in[:, o2:].astype(BF16),
        "qg": row(p["q_norm_g"]), "kvg": row(p["kv_norm_g"]),
        "wuqn": w_uq[:, :, :nope].reshape(q_rank, n_heads * nope).astype(BF16),
        "wuqr": _rope_group(w_uq[:, :, nope:], rope_dim).reshape(q_rank, n_heads * LANES).astype(BF16),
        "wuk": _block_diag(jnp.transpose(p["w_uk"], (1, 2, 0))).astype(BF16),
        "invf": invf,
        "mu": row(p["shift_mu"]), "w0": row(p["w0"]), "a0": row(p["a0"]),
        "wl": jnp.concatenate([p["w_lora_up"], jnp.zeros((d_al, rw_dim), F32)], axis=0).astype(BF16),
        "al": jnp.concatenate([jnp.zeros((d_wl, rw_dim), F32), p["a_lora_up"]], axis=0).astype(BF16),
        "gl": p["g_lora_up"].astype(BF16),
        "kk": row(p["k_k"]), "ka": row(p["k_a"]),
        "hones": ones_blk(rw_dim), "pones": ones_blk(LANES),
        "rk": row(p["r_k"]), "lg": row(p["lnx_g"]), "lb": row(p["lnx_b"]),
        "wuv": _block_diag(jnp.transpose(p["w_uv"], (1, 0, 2))).astype(BF16),
        "woa": w_out[:d_mla].astype(BF16), "wob": w_out[d_mla:].astype(BF16),
    }


def _pair_state(s):
    b, h, n, _ = s.shape
    s = s.reshape(b, h // HEAD_PAIR, HEAD_PAIR, n, n)
    eye = jnp.eye(HEAD_PAIR, dtype=s.dtype)
    return (s[:, :, :, :, None, :] * eye[None, None, :, None, :, None]).reshape(b, h // HEAD_PAIR, HEAD_PAIR * n,
                                                                                 HEAD_PAIR * n)


def _unpair_state(s, n):
    b, hp = s.shape[:2]
    s = s.reshape(b, hp, HEAD_PAIR, n, HEAD_PAIR, n)
    return jnp.stack([s[:, :, i, :, i, :] for i in range(HEAD_PAIR)], axis=2).reshape(b, hp * HEAD_PAIR, n, n)


def _group(x, shift_buf, wkv0, wts, ffa, ffb, lns, dims, *, pos0, attend, tm_in, chunk, alpha):
    n_seq, seq, d = x.shape
    n = n_seq * seq
    n_heads, rw_dim, rope_dim = dims["n_heads"], dims["rw_dim"], dims["rope_dim"]
    x1 = ffn_ln(x.reshape(n, d), *ffa, lns["ln1_g"], lns["ln1_b"], alpha=alpha)
    (q, kvcat, ckv, kr, r, lw, k, v, kk, asig, g, shift_out) = mixer_in(
        x1, shift_buf.reshape(n_seq, 1, -1), wts, n_seq=n_seq, seq=seq, pos0=pos0, tm=tm_in, n_heads=n_heads,
        sm_scale=dims["sm_scale"], rw_dim=rw_dim, rope_dim=rope_dim)
    ocat = attend(q.reshape(q.shape[0], n_heads * q.shape[2], q.shape[3]), kvcat.reshape(n_seq, seq, -1))
    yrw, s_out = rwkv_chunked(r, lw, k, v, kk, asig, g, _pair_state(wkv0), wts["rk"], wts["lg"], wts["lb"],
                              wts["pones"], n_seq=n_seq, seq=seq, chunk=chunk)
    y = out_ffn_ln(x1, ocat.reshape(n, -1), yrw, wts["wuv"], wts["woa"], wts["wob"], lns["ln2_g"], lns["ln2_b"],
                   *ffb, lns["ln3_g"], lns["ln3_b"], alpha=alpha)
    return (y.reshape(n_seq, seq, d), ckv.reshape(1, n_seq, seq, -1), kr.reshape(1, n_seq, seq, -1),
            _unpair_state(s_out, rw_dim // dims["rw_heads"])[None], shift_out.reshape(1, n_seq, -1))


def kernel(x_prompt, x_sample, cache_ckv, cache_krope, state_wkv, state_shift, page_table, ln1_g, ln1_b, ffa_w_gate, ffa_w_up, ffa_w_down, w_in, q_norm_g, w_uq, kv_norm_g, w_uk, w_uv, shift_mu, w0, w_lora_up, a0, a_lora_up, g_lora_up, k_k, k_a, r_k, lnx_g, lnx_b, w_out, ln2_g, ln2_b, ffb_w_gate, ffb_w_up, ffb_w_down, ln3_g, ln3_b):
    depth = w_in.shape[0]
    assert depth == 1
    alpha = (2 * depth) ** 0.25
    n_heads, nope_rope = w_uq.shape[2], w_uq.shape[3]
    rope_dim = cache_krope.shape[-1]
    rw_heads, rw_head = state_wkv.shape[2], state_wkv.shape[3]
    dims = dict(q_rank=w_uq.shape[1], kv_rank=w_uk.shape[1], rope_dim=rope_dim, n_heads=n_heads,
                nope=nope_rope - rope_dim, rw_dim=rw_heads * rw_head, rw_heads=rw_heads, mla_v=w_uv.shape[3],
                sm_scale=float(nope_rope) ** -0.5)
    p = dict(w_in=w_in[0], q_norm_g=q_norm_g[0], w_uq=w_uq[0], kv_norm_g=kv_norm_g[0], w_uk=w_uk[0], w_uv=w_uv[0],
             shift_mu=shift_mu[0], w0=w0[0], w_lora_up=w_lora_up[0], a0=a0[0], a_lora_up=a_lora_up[0],
             g_lora_up=g_lora_up[0], k_k=k_k[0], k_a=k_a[0], r_k=r_k[0], lnx_g=lnx_g[0], lnx_b=lnx_b[0],
             w_out=w_out[0])
    wts = _prep_weights(p, dims)
    row = lambda a: a[0].reshape(1, -1)
    lns = dict(ln1_g=row(ln1_g), ln1_b=row(ln1_b), ln2_g=row(ln2_g), ln2_b=row(ln2_b), ln3_g=row(ln3_g),
               ln3_b=row(ln3_b))
    ffa = (ffa_w_gate[0].astype(BF16), ffa_w_up[0].astype(BF16), ffa_w_down[0].astype(BF16))
    ffb = (ffb_w_gate[0].astype(BF16), ffb_w_up[0].astype(BF16), ffb_w_down[0].astype(BF16))

    bp, sp, _ = x_prompt.shape
    bs, ss, _ = x_sample.shape
    n_pages, page = page_table.shape[1], cache_ckv.shape[2]
    past_len = n_pages * page
    tq = min(256, sp)

    attend_p = functools.partial(prompt_attn, n_seq=bp, seq=sp, tq=tq, n_heads=n_heads)
    out_p = _group(x_prompt, jnp.zeros((bp, state_shift.shape[-1]), F32),
                   jnp.zeros((bp,) + state_wkv.shape[2:], F32), wts, ffa, ffb, lns, dims,
                   pos0=0.0, attend=attend_p, tm_in=tq, chunk=min(64, sp), alpha=alpha)

    kr_view = cache_krope[0].reshape(cache_krope.shape[1], page * rope_dim // LANES, LANES)
    attend_s = lambda q, kvnew: sample_attn(q, kvnew, cache_ckv[0], kr_view, page_table, dec_seq=ss,
                                            n_heads=n_heads, rope_dim=rope_dim,
                                            pages_per_step=math.gcd(n_pages, 8))
    out_s = _group(x_sample, state_shift[0], state_wkv[0], wts, ffa, ffb, lns, dims,
                   pos0=float(past_len), attend=attend_s, tm_in=min(128, bs * ss), chunk=ss, alpha=alpha)
    return (out_p[0], out_s[0]) + out_p[1:] + out_s[1:]
```

```python
import functools
import math

import jax
import jax.numpy as jnp
from jax import lax
from jax.experimental import pallas as pl
from jax.experimental.pallas import tpu as pltpu

F32 = jnp.float32
BF16 = jnp.bfloat16

LANES = 128
HEAD_PAIR = 2
LN_EPS = 1e-5
RMS_EPS = 1e-6
GN_EPS = 64e-5
ROPE_BASE = 10000.0
NEG = -1e30
VMEM_LIMIT = 56 * 1024 * 1024


def _dot(a, b):
    return jnp.dot(a, b, preferred_element_type=F32)


def _dot_nt(a, b):
    return lax.dot_general(a, b, (((1,), (1,)), ((), ())), preferred_element_type=F32)


def _dot_tn(a, b):
    return lax.dot_general(a, b, (((0,), (0,)), ((), ())), preferred_element_type=F32)


def _layernorm(y, g, b):
    mu = jnp.mean(y, axis=-1, keepdims=True)
    d = y - mu
    var = jnp.mean(d * d, axis=-1, keepdims=True)
    return d * lax.rsqrt(var + LN_EPS) * g + b


def _rmsnorm(y, g):
    return y * lax.rsqrt(jnp.mean(y * y, axis=-1, keepdims=True) + RMS_EPS) * g


def _sigmoid(x):
    return 1.0 / (1.0 + jnp.exp(-x))


def _split_dot(x, w_bf16):
    hi = x.astype(BF16)
    lo = (x - hi.astype(F32)).astype(BF16)
    return _dot(hi, w_bf16) + _dot(lo, w_bf16)


def _const_spec(shape):
    nd = len(shape)
    return pl.BlockSpec(shape, lambda *_: (0,) * nd, pipeline_mode=pl.Buffered(1))


def _swiglu_ln(x, wg_ref, wu_ref, wd_ref, g_ref, b_ref, alpha, ff_chunk):
    xb = x.astype(BF16)
    d_ff = wg_ref.shape[1]
    acc = jnp.zeros(x.shape, F32)
    for c in range(d_ff // ff_chunk):
        sl = slice(c * ff_chunk, (c + 1) * ff_chunk)
        gate = _dot(xb, wg_ref[:, sl])
        up = _dot(xb, wu_ref[:, sl])
        h = (gate * _sigmoid(gate) * up).astype(BF16)
        acc = acc + _dot(h, wd_ref[sl, :])
    return _layernorm(alpha * x + 0.5 * acc, g_ref[...], b_ref[...])


def _ffn_ln_kernel(x_ref, wg_ref, wu_ref, wd_ref, g_ref, b_ref, o_ref, *, alpha, ff_chunk):
    o_ref[...] = _swiglu_ln(x_ref[...], wg_ref, wu_ref, wd_ref, g_ref, b_ref, alpha, ff_chunk)


def _out_ffn_ln_kernel(x_ref, ocat_ref, yrw_ref, wuv_ref, woa_ref, wob_ref, g2_ref, b2_ref,
                       wg_ref, wu_ref, wd_ref, g3_ref, b3_ref, o_ref, *, alpha, ff_chunk):
    y_mla = _dot(ocat_ref[...], wuv_ref[...]).astype(BF16)
    y = _dot(y_mla, woa_ref[...]) + _dot(yrw_ref[...], wob_ref[...])
    x2 = _layernorm(alpha * x_ref[...] + y, g2_ref[...], b2_ref[...])
    o_ref[...] = _swiglu_ln(x2, wg_ref, wu_ref, wd_ref, g3_ref, b3_ref, alpha, ff_chunk)


def _row_tile(n, want):
    t = min(n, want)
    assert n % t == 0
    return t


def _ffn_chunk(d_ff):
    for c in (512, 256, 128):
        if d_ff % c == 0:
            return c
    return d_ff


def ffn_ln(x, wg, wu, wd, g, b, *, alpha, tm=512):
    n, d = x.shape
    tm = _row_tile(n, tm)
    kern = functools.partial(_ffn_ln_kernel, alpha=alpha, ff_chunk=_ffn_chunk(wg.shape[1]))
    row = pl.BlockSpec((tm, d), lambda i: (i, 0))
    return pl.pallas_call(
        kern, grid=(n // tm,),
        in_specs=[row, _const_spec(wg.shape), _const_spec(wu.shape), _const_spec(wd.shape),
                  _const_spec(g.shape), _const_spec(b.shape)],
        out_specs=row, out_shape=jax.ShapeDtypeStruct((n, d), F32),
        compiler_params=pltpu.CompilerParams(dimension_semantics=("arbitrary",), vmem_limit_bytes=VMEM_LIMIT),
        name="ffn_ln",
    )(x, wg, wu, wd, g, b)


def out_ffn_ln(x, ocat, yrw, wuv, woa, wob, g2, b2, wg, wu, wd, g3, b3, *, alpha, tm=512):
    n, d = x.shape
    tm = _row_tile(n, tm)
    kern = functools.partial(_out_ffn_ln_kernel, alpha=alpha, ff_chunk=_ffn_chunk(wg.shape[1]))
    row = lambda w: pl.BlockSpec((tm, w), lambda i: (i, 0))
    consts = (wuv, woa, wob, g2, b2, wg, wu, wd, g3, b3)
    return pl.pallas_call(
        kern, grid=(n // tm,),
        in_specs=[row(d), row(ocat.shape[1]), row(yrw.shape[1])] + [_const_spec(c.shape) for c in consts],
        out_specs=row(d), out_shape=jax.ShapeDtypeStruct((n, d), F32),
        compiler_params=pltpu.CompilerParams(dimension_semantics=("arbitrary",), vmem_limit_bytes=VMEM_LIMIT),
        name="out_ffn_ln",
    )(x, ocat, yrw, *consts)


def _mixer_in_kernel(x_ref, sb_ref, wq_ref, wkv_ref, wkr_ref, wrw_ref, qg_ref, kvg_ref, wuqn_ref, wuqr_ref,
                     wuk_ref, invf_ref, mu_ref, w0_ref, wl_ref, a0_ref, al_ref, gl_ref, kk_ref, ka_ref,
                     hones_ref,
                     q_out, kvcat_out, ckv_out, kr_out, r_out, lw_out, k_out, v_out, kkn_out, as_out, g_out,
                     sh_out, rw_sc, *, seq, tm, pos0, n_heads, q_scale, rw_dim, rope_dim):
    t = pl.program_id(1)
    xb = x_ref[...].astype(BF16)
    whole_seqs = tm >= seq

    row = lax.broadcasted_iota(jnp.int32, (tm, 1), 0)
    row_in_seq = (row % seq) if whole_seqs else (row + t * tm)
    pos = row_in_seq.astype(F32) + pos0
    ang = pos * invf_ref[...]
    lane = lax.broadcasted_iota(jnp.int32, (tm, LANES), 1)
    cosm = jnp.where(lane < rope_dim, jnp.cos(ang), 0.0)
    sinm = jnp.where(lane < rope_dim, jnp.sin(ang), 0.0)

    def rope(grp):
        return grp * cosm + pltpu.roll(grp, LANES - rope_dim, axis=1) * sinm

    ckv = _rmsnorm(_dot(xb, wkv_ref[...]), kvg_ref[...])
    kr = rope(_dot(xb, wkr_ref[...]))
    ckv_out[...] = ckv
    kr_out[...] = kr[:, :rope_dim]
    kvcat_out[:, :LANES] = ckv.astype(BF16)
    kvcat_out[:, LANES:] = kr.astype(BF16)

    cq = _rmsnorm(_dot(xb, wq_ref[...]), qg_ref[...]).astype(BF16)
    q_lat = _dot(_dot(cq, wuqn_ref[...]).astype(BF16), wuk_ref[...])
    q_rg = _dot(cq, wuqr_ref[...])
    nbg, rows = q_out.shape[0], q_out.shape[2]
    for h in range(n_heads):
        sl = slice(h * LANES, (h + 1) * LANES)
        q_h = jnp.concatenate([q_lat[:, sl] * q_scale, rope(q_rg[:, sl]) * q_scale], axis=-1).astype(BF16)
        q_out[:, h] = q_h.reshape(nbg, rows, 2 * LANES)

    rw = _dot(xb, wrw_ref[...])
    rw_sc[8:tm + 8, :] = rw
    if whole_seqs:
        nb = tm // seq
        rw_sc[7:8, :] = jnp.zeros((1, rw.shape[1]), F32)
        prev = rw_sc[7:tm + 7, :]
        first = jnp.broadcast_to(sb_ref[...], (nb, seq, rw.shape[1])).reshape(tm, rw.shape[1])
        prev = jnp.where(row % seq == 0, first, prev)
        sh_out[...] = rw.reshape(nb, seq, rw.shape[1])[:, seq - 1:seq, :]
    else:
        @pl.when(t == 0)
        def _():
            rw_sc[7:8, :] = sb_ref[0]
        prev = rw_sc[7:tm + 7, :]
        rw_sc[7:8, :] = rw[tm - 1:tm, :]
        sh_out[0] = rw[tm - 1:tm, :]
    rws = rw + (prev - rw) * mu_ref[...]

    r = rws[:, :rw_dim]
    k = rws[:, rw_dim:2 * rw_dim]
    v = rws[:, 2 * rw_dim:3 * rw_dim]
    lora_in = rws[:, 3 * rw_dim:3 * rw_dim + LANES]
    dg = rws[:, 3 * rw_dim + LANES:]
    z = w0_ref[...] + _dot(jnp.tanh(lora_in).astype(BF16), wl_ref[...])
    nz = -z
    softplus = jnp.maximum(nz, 0.0) + jnp.log(1.0 + jnp.exp(-jnp.abs(nz)))
    w = -softplus - 0.5
    a = _sigmoid(a0_ref[...] + _dot(lora_in.astype(BF16), al_ref[...]))
    g = _dot(_sigmoid(dg).astype(BF16), gl_ref[...])
    kk = k * kk_ref[...]
    norm = jnp.sqrt(_split_dot(kk * kk, hones_ref[...]))
    kk = kk / jnp.maximum(norm, 1e-12)
    r_out[...] = r
    lw_out[...] = -jnp.exp(w)
    k_out[...] = k * (1.0 + (a - 1.0) * ka_ref[...])
    v_out[...] = v
    kkn_out[...] = kk
    as_out[...] = a
    g_out[...] = g


def mixer_in(x, shift_buf, wts, *, n_seq, seq, pos0, tm, n_heads, q_scale, rw_dim, rope_dim):
    n, d = x.shape
    rw_cols = wts["wrw"].shape[1]
    if tm >= seq:
        assert tm % seq == 0 and n % tm == 0
        nb = tm // seq
        grid = (n // tm, 1)
        rows = seq
        nbg = nb
        row_idx = lambda i, t: (i, 0)
        sb_spec = pl.BlockSpec((nb, 1, rw_cols), lambda i, t: (i, 0, 0))
        q_spec = pl.BlockSpec((nb, n_heads, seq, 2 * LANES), lambda i, t: (i, 0, 0, 0))
        q_shape = (n_seq, n_heads, seq, 2 * LANES)
    else:
        assert seq % tm == 0
        tps = seq // tm
        grid = (n_seq, tps)
        rows = tm
        nbg = 1
        row_idx = lambda i, t: (i * tps + t, 0)
        sb_spec = pl.BlockSpec((1, 1, rw_cols), lambda i, t: (i, 0, 0))
        q_spec = pl.BlockSpec((1, n_heads, tm, 2 * LANES), lambda i, t: (i * tps + t, 0, 0, 0))
        q_shape = (n_seq * tps, n_heads, tm, 2 * LANES)
    rowspec = lambda w: pl.BlockSpec((tm, w), row_idx)
    names = ("wq", "wkv", "wkr", "wrw", "qg", "kvg", "wuqn", "wuqr", "wuk", "invf", "mu", "w0", "wl", "a0",
             "al", "gl", "kk", "ka", "hones")
    consts = [wts[k] for k in names]
    kern = functools.partial(_mixer_in_kernel, seq=seq, tm=tm, pos0=float(pos0), n_heads=n_heads,
                             q_scale=q_scale, rw_dim=rw_dim, rope_dim=rope_dim)
    f = lambda w, dt=F32: jax.ShapeDtypeStruct((n, w), dt)
    out_shape = (jax.ShapeDtypeStruct(q_shape, BF16), f(2 * LANES, BF16), f(LANES), f(rope_dim)) \
        + (f(rw_dim),) * 7 + (jax.ShapeDtypeStruct((n_seq, 1, rw_cols), F32),)
    out_specs = (q_spec, rowspec(2 * LANES), rowspec(LANES), rowspec(rope_dim)) + (rowspec(rw_dim),) * 7 + (sb_spec,)
    return pl.pallas_call(
        kern, grid=grid,
        in_specs=[rowspec(d), sb_spec] + [_const_spec(c.shape) for c in consts],
        out_specs=out_specs, out_shape=out_shape,
        scratch_shapes=[pltpu.VMEM((tm + 8, rw_cols), F32)],
        compiler_params=pltpu.CompilerParams(dimension_semantics=("arbitrary", "arbitrary"),
                                             vmem_limit_bytes=VMEM_LIMIT),
        name="mixer_in",
    )(x, shift_buf, *consts)


def _prompt_attn_kernel(q_ref, kv_ref, o_ref, m_sc, acc_sc, s_sc, *, tq, tk, n_heads):
    qb = pl.program_id(1)
    q = q_ref[0]
    m_sc[...] = jnp.full(m_sc.shape, NEG, F32)
    acc_sc[...] = jnp.zeros(acc_sc.shape, F32)
    ones = jnp.ones((tk, LANES), BF16)

    def scores(start, masked):
        s = _dot_nt(q, kv_ref[0, pl.ds(start, tk), :])
        if masked:
            qi = lax.broadcasted_iota(jnp.int32, s.shape, 0) % tq + qb * tq
            kj = lax.broadcasted_iota(jnp.int32, s.shape, 1) + start
            s = jnp.where(qi >= kj, s, NEG)
        return s

    def consume(s, start):
        m_prev = m_sc[...]
        m_new = jnp.maximum(m_prev, jnp.max(s, axis=-1, keepdims=True))
        corr = jnp.exp2(m_prev - m_new)
        p = jnp.exp2(s - jnp.tile(m_new, (1, tk // LANES))).astype(BF16)
        v_ext = jnp.concatenate([kv_ref[0, pl.ds(start, tk), :LANES], ones], axis=-1)
        acc_sc[...] = acc_sc[...] * jnp.tile(corr, (1, 2)) + _dot(p, v_ext)
        m_sc[...] = m_new

    n_full = (qb * tq) // tk
    blk = lambda j: pl.multiple_of(j * tk, tk)

    @pl.when(n_full == 0)
    def _():
        s_sc[...] = scores(0, True)

    @pl.when(n_full > 0)
    def _():
        s_sc[...] = scores(0, False)

    def body(j, carry):
        s = s_sc[...]
        s_next = scores(blk(j + 1), False)
        consume(s, blk(j))
        s_sc[...] = s_next
        return carry

    lax.fori_loop(0, n_full - 1, body, 0)

    @pl.when(n_full > 0)
    def _():
        s = s_sc[...]
        s_next = scores(blk(n_full), True)
        consume(s, blk(n_full - 1))
        s_sc[...] = s_next

    consume(s_sc[...], blk(n_full))
    acc = acc_sc[...]
    o = (acc[:, :LANES] / acc[:, LANES:]).astype(BF16)
    for h in range(n_heads):
        o_ref[0, :, h * LANES:(h + 1) * LANES] = o[h * tq:(h + 1) * tq]


def prompt_attn(q, kvcat, *, n_seq, seq, tq, n_heads):
    nq = seq // tq
    m = n_heads * tq
    tk = 2 * tq if seq % (2 * tq) == 0 else tq
    kern = functools.partial(_prompt_attn_kernel, tq=tq, tk=tk, n_heads=n_heads)
    return pl.pallas_call(
        kern, grid=(n_seq, nq),
        in_specs=[pl.BlockSpec((1, m, 2 * LANES), lambda b, i: (b * nq + i, 0, 0)),
                  pl.BlockSpec((1, seq, 2 * LANES), lambda b, i: (b, 0, 0))],
        out_specs=pl.BlockSpec((1, tq, n_heads * LANES), lambda b, i: (b, i, 0)),
        out_shape=jax.ShapeDtypeStruct((n_seq, seq, n_heads * LANES), BF16),
        scratch_shapes=[pltpu.VMEM((m, LANES), F32), pltpu.VMEM((m, 2 * LANES), F32), pltpu.VMEM((m, tk), F32)],
        compiler_params=pltpu.CompilerParams(dimension_semantics=("arbitrary", "arbitrary"),
                                             vmem_limit_bytes=VMEM_LIMIT),
        name="prompt_attn",
    )(q, kvcat)


def _softmax_update(s, vals, m_sc, l_sc, acc_sc):
    m_prev = m_sc[...]
    m_new = jnp.maximum(m_prev, jnp.max(s, axis=-1, keepdims=True))
    corr = jnp.exp2(m_prev - m_new)
    p = jnp.exp2(s - m_new)
    l_sc[...] = l_sc[...] * corr + jnp.sum(p, axis=-1, keepdims=True)
    acc_sc[...] = acc_sc[...] * corr + _dot(p.astype(BF16), vals)
    m_sc[...] = m_new


def _sample_attn_kernel(pt_ref, q_ref, kvnew_ref, *refs, pages_per_step, rope_dim, dec_seq, n_heads):
    ckv_refs = refs[:pages_per_step]
    kr_refs = refs[pages_per_step:2 * pages_per_step]
    o_ref, m_sc, l_sc, acc_sc = refs[2 * pages_per_step:]
    g = pl.program_id(1)
    q = q_ref[0]
    q_lat, q_rope = q[:, :LANES], q[:, LANES:LANES + rope_dim]

    @pl.when(g == 0)
    def _():
        m_sc[...] = jnp.full(m_sc.shape, NEG, F32)
        l_sc[...] = jnp.zeros(l_sc.shape, F32)
        acc_sc[...] = jnp.zeros(acc_sc.shape, F32)

    c_parts, s_parts = [], []
    for ckv_ref, kr_ref in zip(ckv_refs, kr_refs):
        c = ckv_ref[...].astype(BF16)
        c_parts.append(c)
        s_parts.append(_dot_nt(q_lat, c) + _dot(q_rope, kr_ref[...].astype(BF16)))
    _softmax_update(jnp.concatenate(s_parts, axis=-1), jnp.concatenate(c_parts, axis=0), m_sc, l_sc, acc_sc)

    @pl.when(g == pl.num_programs(1) - 1)
    def _():
        kn = kvnew_ref[0]
        s = _dot_nt(q, kn)
        qi = lax.broadcasted_iota(jnp.int32, s.shape, 0) % dec_seq
        kj = lax.broadcasted_iota(jnp.int32, s.shape, 1)
        s = jnp.where(qi >= kj, s, NEG)
        _softmax_update(s, kn[:, :LANES], m_sc, l_sc, acc_sc)
        o = (acc_sc[...] / l_sc[...]).astype(BF16)
        for h in range(n_heads):
            o_ref[0, :, h * LANES:(h + 1) * LANES] = o[h * dec_seq:(h + 1) * dec_seq]


def sample_attn(q, kvnew, cache_ckv, cache_kr_t, page_table, *, dec_seq, n_heads, pages_per_step):
    b, n_pages = page_table.shape
    page = cache_ckv.shape[1]
    rope_dim = cache_kr_t.shape[1]
    pps = pages_per_step
    assert n_pages % pps == 0 and page == LANES
    m = n_heads * dec_seq
    kern = functools.partial(_sample_attn_kernel, pages_per_step=pps, rope_dim=rope_dim, dec_seq=dec_seq,
                             n_heads=n_heads)

    def page_spec(rows, j):
        return pl.BlockSpec((None, rows, LANES), lambda i, g, pt: (pt[i * n_pages + g * pps + j], 0, 0))

    grid_spec = pltpu.PrefetchScalarGridSpec(
        num_scalar_prefetch=1, grid=(b, n_pages // pps),
        in_specs=[pl.BlockSpec((1, m, 2 * LANES), lambda i, g, pt: (i, 0, 0)),
                  pl.BlockSpec((1, dec_seq, 2 * LANES), lambda i, g, pt: (i, 0, 0))]
        + [page_spec(page, j) for j in range(pps)] + [page_spec(rope_dim, j) for j in range(pps)],
        out_specs=pl.BlockSpec((1, dec_seq, n_heads * LANES), lambda i, g, pt: (i, 0, 0)),
        scratch_shapes=[pltpu.VMEM((m, 1), F32), pltpu.VMEM((m, 1), F32), pltpu.VMEM((m, LANES), F32)])
    return pl.pallas_call(
        kern, grid_spec=grid_spec,
        out_shape=jax.ShapeDtypeStruct((b, dec_seq, n_heads * LANES), BF16),
        compiler_params=pltpu.CompilerParams(dimension_semantics=("arbitrary", "arbitrary"),
                                             vmem_limit_bytes=VMEM_LIMIT),
        name="sample_attn",
    )(page_table.reshape(-1), q, kvnew, *([cache_ckv] * pps), *([cache_kr_t] * pps))


def _cumsum_rows(tri, x):
    if x.shape[0] < 16:
        return jnp.dot(tri.astype(F32), x, preferred_element_type=F32, precision=lax.Precision.HIGHEST)
    hi = x.astype(BF16)
    lo = (x - hi.astype(F32)).astype(BF16)
    return _dot(tri, hi) + _dot(tri, lo)


def _rwkv_chunk(rs, lws, ks, vs, kks, asigs, s0s, tri, strict, incl, head0):
    n = range(len(rs))
    c = rs[0].shape[0]
    c2 = 2 * c

    def stack(x):
        return jnp.concatenate([jnp.where(head0, x, 0.0), jnp.where(head0, 0.0, x)], axis=0)

    cs = [_cumsum_rows(tri, lws[i]) for i in n]
    cs_end = [cs[i][c - 1:c, :] for i in n]
    e_neg = [jnp.exp(-cs[i]) for i in n]
    e_end = [jnp.exp(cs_end[i] - cs[i]) for i in n]
    bvec = [kks[i] * asigs[i] for i in n]
    a_t = [stack(-kks[i] * jnp.exp(cs[i] - lws[i])) for i in n]
    r_t = [stack(rs[i] * jnp.exp(cs[i])).astype(BF16) for i in n]
    k_t = [stack(ks[i] * e_neg[i]) for i in n]
    b_t = [stack(bvec[i] * e_neg[i]) for i in n]
    k_e = [stack(ks[i] * e_end[i]).astype(BF16) for i in n]
    b_e = [stack(bvec[i] * e_end[i]).astype(BF16) for i in n]
    v_s = [stack(vs[i]).astype(BF16) for i in n]

    sc = [_dot_nt(jnp.concatenate([a_t[i].astype(BF16), r_t[i]], axis=0),
                  jnp.concatenate([k_t[i], b_t[i]], axis=0).astype(BF16)) for i in n]
    l_ak = [jnp.where(strict, sc[i][:c2, :c2], 0.0).astype(BF16) for i in n]
    lp = [jnp.where(strict, sc[i][:c2, c2:], 0.0).astype(BF16) for i in n]
    a_rk = [jnp.where(incl, sc[i][c2:, :c2], 0.0).astype(BF16) for i in n]
    a_rb = [jnp.where(incl, sc[i][c2:, c2:], 0.0).astype(BF16) for i in n]

    x = [jnp.concatenate([a_t[i], _dot(l_ak[i], v_s[i])], axis=-1) for i in n]
    n_fac = int(math.log2(c))
    for f in range(n_fac):
        x = [x[i] + _dot(lp[i], x[i].astype(BF16)) for i in n]
        if f + 1 < n_fac:
            lp = [_dot(lp[i], lp[i]).astype(BF16) for i in n]

    s0b = [s0s[i].astype(BF16) for i in n]
    u = [(_dot_nt(x[i][:, :LANES].astype(BF16), s0b[i]) + x[i][:, LANES:]).astype(BF16) for i in n]
    y_st = [_dot_nt(r_t[i], s0b[i]) + _dot(a_rk[i], v_s[i]) + _dot(a_rb[i], u[i]) for i in n]
    ys = [y_st[i][:c] + y_st[i][c:] for i in n]
    s_new = [s0s[i] * jnp.exp(cs_end[i]) + _dot_tn(v_s[i], k_e[i]) + _dot_tn(u[i], b_e[i]) for i in n]
    return ys, s_new


def _rwkv_kernel(r_ref, lw_ref, k_ref, v_ref, kk_ref, as_ref, g_ref, s0_ref, rk_ref, lg_ref, lb_ref, pones_ref,
                 y_ref, sout_ref, s_sc, *, chunk, n_pairs, n_seq_blk):
    ci = pl.program_id(1)

    @pl.when(ci == 0)
    def _():
        s_sc[...] = s0_ref[...]

    c = chunk
    ti = lax.broadcasted_iota(jnp.int32, (c, c), 0)
    tj = lax.broadcasted_iota(jnp.int32, (c, c), 1)
    tri = (ti >= tj).astype(BF16)
    si = lax.broadcasted_iota(jnp.int32, (2 * c, 2 * c), 0)
    sj = lax.broadcasted_iota(jnp.int32, (2 * c, 2 * c), 1)
    strict = si > sj
    incl = si >= sj
    head0 = lax.broadcasted_iota(jnp.int32, (c, LANES), 1) < (LANES // HEAD_PAIR)
    pones = pones_ref[...]
    inv_n = 1.0 / (LANES // HEAD_PAIR)

    streams = [(b, p) for b in range(n_seq_blk) for p in range(n_pairs)]
    n = range(len(streams))
    lanes = [slice(p * LANES, (p + 1) * LANES) for _, p in streams]
    get = lambda ref: [ref[b, :, lanes[i]] for i, (b, _) in enumerate(streams)]
    par = lambda ref: [ref[:, lanes[i]] for i in n]
    rs, ks, vs = get(r_ref), get(k_ref), get(v_ref)
    ys, s_new = _rwkv_chunk(rs, get(lw_ref), ks, vs, get(kk_ref), get(as_ref),
                            [s_sc[b, p] for b, p in streams], tri, strict, incl, head0)
    for i, (b, p) in enumerate(streams):
        s_sc[b, p] = s_new[i]
    lg, lb, rk = par(lg_ref), par(lb_ref), par(rk_ref)
    mu = [_split_dot(ys[i], pones) * inv_n for i in n]
    d = [ys[i] - mu[i] for i in n]
    var = [_split_dot(d[i] * d[i], pones) * inv_n for i in n]
    bonus = [_split_dot(rs[i] * ks[i] * rk[i], pones) * vs[i] for i in n]
    gs = get(g_ref)
    for i, (b, p) in enumerate(streams):
        o = d[i] * lax.rsqrt(var[i] + GN_EPS) * lg[i] + lb[i] + bonus[i]
        y_ref[b, :, lanes[i]] = (o * gs[i]).astype(BF16)

    @pl.when(ci == pl.num_programs(1) - 1)
    def _():
        sout_ref[...] = s_sc[...]


def rwkv_chunked(r, lw, k, v, kk, asig, g, s0, rk, lg, lb, pones, *, chunk, n_seq_blk):
    n_seq, seq, rw_dim = r.shape
    n_pairs = rw_dim // LANES
    nc = seq // chunk
    assert n_seq % n_seq_blk == 0 and seq % chunk == 0
    kern = functools.partial(_rwkv_kernel, chunk=chunk, n_pairs=n_pairs, n_seq_blk=n_seq_blk)
    row = pl.BlockSpec((n_seq_blk, chunk, rw_dim), lambda b, c: (b, c, 0))
    st = pl.BlockSpec((n_seq_blk, n_pairs, LANES, LANES), lambda b, c: (b, 0, 0, 0))
    return pl.pallas_call(
        kern, grid=(n_seq // n_seq_blk, nc),
        in_specs=[row] * 7 + [st] + [_const_spec(x.shape) for x in (rk, lg, lb, pones)],
        out_specs=(row, st),
        out_shape=(jax.ShapeDtypeStruct((n_seq, seq, rw_dim), BF16), jax.ShapeDtypeStruct(s0.shape, F32)),
        scratch_shapes=[pltpu.VMEM((n_seq_blk, n_pairs, LANES, LANES), F32)],
        compiler_params=pltpu.CompilerParams(dimension_semantics=("arbitrary", "arbitrary"),
                                             vmem_limit_bytes=VMEM_LIMIT),
        name="rwkv_chunked",
    )(r, lw, k, v, kk, asig, g, s0, rk, lg, lb, pones)


def _block_diag(blocks):
    n, r, c = blocks.shape
    eye = jnp.eye(n, dtype=blocks.dtype)
    return (eye[:, None, :, None] * blocks[:, :, None, :]).reshape(n * r, n * c)


def _rope_group(w, rope_dim):
    half = rope_dim // 2
    x1, x2 = w[..., :half], w[..., half:]
    pad = jnp.zeros(w.shape[:-1] + (LANES - 2 * rope_dim,), w.dtype)
    return jnp.concatenate([x1, x2, -x2, x1, pad], axis=-1)


def _prep_weights(p, dims):
    q_rank, kv_rank, rope_dim, n_heads, nope, rw_dim = (dims[k] for k in
                                                        ("q_rank", "kv_rank", "rope_dim", "n_heads", "nope", "rw_dim"))
    w_in = p["w_in"]
    o1, o2 = q_rank + kv_rank, q_rank + kv_rank + rope_dim
    w_uq = p["w_uq"]
    row = lambda a: a.reshape(1, -1).astype(F32)
    d_wl = p["w_lora_up"].shape[0]
    d_al = p["a_lora_up"].shape[0]
    assert d_wl + d_al == LANES
    half = jnp.arange(0, rope_dim, 2, dtype=F32)
    inv = ROPE_BASE ** (-half / rope_dim)
    invf = jnp.concatenate([inv, inv, jnp.zeros((LANES - rope_dim,), F32)]).reshape(1, LANES)
    rw_head = rw_dim // dims["rw_heads"]
    ones_blk = lambda width: _block_diag(jnp.ones((width // rw_head, rw_head, rw_head), F32)).astype(BF16)
    w_out = p["w_out"]
    d_mla = n_heads * dims["mla_v"]
    return {
        "wq": w_in[:, :q_rank].astype(BF16),
        "wkv": w_in[:, q_rank:o1].astype(BF16),
        "wkr": _rope_group(w_in[:, o1:o2], rope_dim).astype(BF16),
        "wrw": w_in[:, o2:].astype(BF16),
        "qg": row(p["q_norm_g"]), "kvg": row(p["kv_norm_g"]),
        "wuqn": w_uq[:, :, :nope].reshape(q_rank, n_heads * nope).astype(BF16),
        "wuqr": _rope_group(w_uq[:, :, nope:], rope_dim).reshape(q_rank, n_heads * LANES).astype(BF16),
        "wuk": _block_diag(jnp.transpose(p["w_uk"], (1, 2, 0))).astype(BF16),
        "invf": invf,
        "mu": row(p["shift_mu"]), "w0": row(p["w0"]), "a0": row(p["a0"]),
        "wl": jnp.concatenate([p["w_lora_up"], jnp.zeros((d_al, rw_dim), F32)], axis=0).astype(BF16),
        "al": jnp.concatenate([jnp.zeros((d_wl, rw_dim), F32), p["a_lora_up"]], axis=0).astype(BF16),
        "gl": p["g_lora_up"].astype(BF16),
        "kk": row(p["k_k"]), "ka": row(p["k_a"]),
        "hones": ones_blk(rw_dim), "pones": ones_blk(LANES),
        "rk": row(p["r_k"]), "lg": row(p["lnx_g"]), "lb": row(p["lnx_b"]),
        "wuv": _block_diag(jnp.transpose(p["w_uv"], (1, 0, 2))).astype(BF16),
        "woa": w_out[:d_mla].astype(BF16), "wob": w_out[d_mla:].astype(BF16),
    }


def _pair_state(s):
    b, h, n, _ = s.shape
    s = s.reshape(b, h // HEAD_PAIR, HEAD_PAIR, n, n)
    eye = jnp.eye(HEAD_PAIR, dtype=s.dtype)
    return (s[:, :, :, :, None, :] * eye[None, None, :, None, :, None]).reshape(b, h // HEAD_PAIR, HEAD_PAIR * n,
                                                                                 HEAD_PAIR * n)


def _unpair_state(s, n):
    b, hp = s.shape[:2]
    s = s.reshape(b, hp, HEAD_PAIR, n, HEAD_PAIR, n)
    return jnp.stack([s[:, :, i, :, i, :] for i in range(HEAD_PAIR)], axis=2).reshape(b, hp * HEAD_PAIR, n, n)


def _group(x, shift_buf, wkv0, wts, ffa, ffb, lns, dims, *, pos0, attend, tm_in, chunk, rw_seq_blk, alpha):
    n_seq, seq, d = x.shape
    n = n_seq * seq
    n_heads, rw_dim, rope_dim = dims["n_heads"], dims["rw_dim"], dims["rope_dim"]
    x1 = ffn_ln(x.reshape(n, d), *ffa, lns["ln1_g"], lns["ln1_b"], alpha=alpha)
    (q, kvcat, ckv, kr, r, lw, k, v, kk, asig, g, shift_out) = mixer_in(
        x1, shift_buf.reshape(n_seq, 1, -1), wts, n_seq=n_seq, seq=seq, pos0=pos0, tm=tm_in, n_heads=n_heads,
        q_scale=dims["q_scale"], rw_dim=rw_dim, rope_dim=rope_dim)
    ocat = attend(q.reshape(q.shape[0], n_heads * q.shape[2], q.shape[3]), kvcat.reshape(n_seq, seq, -1))
    seqs = lambda a: a.reshape(n_seq, seq, -1)
    yrw, s_out = rwkv_chunked(seqs(r), seqs(lw), seqs(k), seqs(v), seqs(kk), seqs(asig), seqs(g), _pair_state(wkv0),
                              wts["rk"], wts["lg"], wts["lb"], wts["pones"], chunk=chunk, n_seq_blk=rw_seq_blk)
    y = out_ffn_ln(x1, ocat.reshape(n, -1), yrw.reshape(n, -1), wts["wuv"], wts["woa"], wts["wob"], lns["ln2_g"], lns["ln2_b"],
                   *ffb, lns["ln3_g"], lns["ln3_b"], alpha=alpha)
    return (y.reshape(n_seq, seq, d), ckv.reshape(1, n_seq, seq, -1), kr.reshape(1, n_seq, seq, -1),
            _unpair_state(s_out, rw_dim // dims["rw_heads"])[None], shift_out.reshape(1, n_seq, -1))


def kernel(x_prompt, x_sample, cache_ckv, cache_krope, state_wkv, state_shift, page_table, ln1_g, ln1_b, ffa_w_gate, ffa_w_up, ffa_w_down, w_in, q_norm_g, w_uq, kv_norm_g, w_uk, w_uv, shift_mu, w0, w_lora_up, a0, a_lora_up, g_lora_up, k_k, k_a, r_k, lnx_g, lnx_b, w_out, ln2_g, ln2_b, ffb_w_gate, ffb_w_up, ffb_w_down, ln3_g, ln3_b):
    depth = w_in.shape[0]
    assert depth == 1
    alpha = (2 * depth) ** 0.25
    n_heads, nope_rope = w_uq.shape[2], w_uq.shape[3]
    rope_dim = cache_krope.shape[-1]
    rw_heads, rw_head = state_wkv.shape[2], state_wkv.shape[3]
    dims = dict(q_rank=w_uq.shape[1], kv_rank=w_uk.shape[1], rope_dim=rope_dim, n_heads=n_heads,
                nope=nope_rope - rope_dim, rw_dim=rw_heads * rw_head, rw_heads=rw_heads, mla_v=w_uv.shape[3],
                q_scale=float(nope_rope) ** -0.5 * math.log2(math.e))
    p = dict(w_in=w_in[0], q_norm_g=q_norm_g[0], w_uq=w_uq[0], kv_norm_g=kv_norm_g[0], w_uk=w_uk[0], w_uv=w_uv[0],
             shift_mu=shift_mu[0], w0=w0[0], w_lora_up=w_lora_up[0], a0=a0[0], a_lora_up=a_lora_up[0],
             g_lora_up=g_lora_up[0], k_k=k_k[0], k_a=k_a[0], r_k=r_k[0], lnx_g=lnx_g[0], lnx_b=lnx_b[0],
             w_out=w_out[0])
    wts = _prep_weights(p, dims)
    row = lambda a: a[0].reshape(1, -1)
    lns = dict(ln1_g=row(ln1_g), ln1_b=row(ln1_b), ln2_g=row(ln2_g), ln2_b=row(ln2_b), ln3_g=row(ln3_g),
               ln3_b=row(ln3_b))
    ffa = (ffa_w_gate[0].astype(BF16), ffa_w_up[0].astype(BF16), ffa_w_down[0].astype(BF16))
    ffb = (ffb_w_gate[0].astype(BF16), ffb_w_up[0].astype(BF16), ffb_w_down[0].astype(BF16))

    bp, sp, _ = x_prompt.shape
    bs, ss, _ = x_sample.shape
    n_pages, page = page_table.shape[1], cache_ckv.shape[2]
    past_len = n_pages * page
    tq = min(256, sp)

    attend_p = functools.partial(prompt_attn, n_seq=bp, seq=sp, tq=tq, n_heads=n_heads)
    out_p = _group(x_prompt, jnp.zeros((bp, state_shift.shape[-1]), F32),
                   jnp.zeros((bp,) + state_wkv.shape[2:], F32), wts, ffa, ffb, lns, dims,
                   pos0=0.0, attend=attend_p, tm_in=tq, chunk=min(64, sp), rw_seq_blk=math.gcd(bp, 4), alpha=alpha)

    kr_t = jnp.swapaxes(cache_krope[0], 1, 2)
    attend_s = lambda q, kvnew: sample_attn(q, kvnew, cache_ckv[0], kr_t, page_table, dec_seq=ss,
                                            n_heads=n_heads, pages_per_step=math.gcd(n_pages, 32))
    out_s = _group(x_sample, state_shift[0], state_wkv[0], wts, ffa, ffb, lns, dims,
                   pos0=float(past_len), attend=attend_s, tm_in=min(128, bs * ss), chunk=ss,
                   rw_seq_blk=math.gcd(bs, 8), alpha=alpha)
    return (out_p[0], out_s[0]) + out_p[1:] + out_s[1:]
```

```python
import functools
import math

import jax
import jax.numpy as jnp
from jax import lax
from jax.experimental import pallas as pl
from jax.experimental.pallas import tpu as pltpu

F32 = jnp.float32
BF16 = jnp.bfloat16

LANES = 128
HEAD_PAIR = 2
LN_EPS = 1e-5
RMS_EPS = 1e-6
GN_EPS = 64e-5
ROPE_BASE = 10000.0
NEG = -1e30
VMEM_LIMIT = 56 * 1024 * 1024


def _dot(a, b):
    return jnp.dot(a, b, preferred_element_type=F32)


def _dot_nt(a, b):
    return lax.dot_general(a, b, (((1,), (1,)), ((), ())), preferred_element_type=F32)


def _dot_tn(a, b):
    return lax.dot_general(a, b, (((0,), (0,)), ((), ())), preferred_element_type=F32)


def _layernorm(y, g, b):
    mu = jnp.mean(y, axis=-1, keepdims=True)
    d = y - mu
    var = jnp.mean(d * d, axis=-1, keepdims=True)
    return d * lax.rsqrt(var + LN_EPS) * g + b


def _rmsnorm(y, g):
    return y * lax.rsqrt(jnp.mean(y * y, axis=-1, keepdims=True) + RMS_EPS) * g


def _sigmoid(x):
    return 1.0 / (1.0 + jnp.exp(-x))


def _head_sums(x, ones2):
    hi = x.astype(BF16)
    lo = (x - hi.astype(F32)).astype(BF16)
    groups = [slice(g * LANES, (g + 1) * LANES) for g in range(x.shape[1] // LANES)]
    return jnp.concatenate([_dot(jnp.concatenate([hi[:, g], lo[:, g]], axis=-1), ones2) for g in groups], axis=-1)


def _const_spec(shape):
    nd = len(shape)
    return pl.BlockSpec(shape, lambda *_: (0,) * nd, pipeline_mode=pl.Buffered(1))


def _swiglu_ln(x, wg_ref, wu_ref, wd_ref, g_ref, b_ref, alpha, ff_chunk):
    xb = x.astype(BF16)
    d_ff = wg_ref.shape[1]
    acc = jnp.zeros(x.shape, F32)
    for c in range(d_ff // ff_chunk):
        sl = slice(c * ff_chunk, (c + 1) * ff_chunk)
        gate = _dot(xb, wg_ref[:, sl])
        up = _dot(xb, wu_ref[:, sl])
        h = (gate * _sigmoid(gate) * up).astype(BF16)
        acc = acc + _dot(h, wd_ref[sl, :])
    return _layernorm(alpha * x + 0.5 * acc, g_ref[...], b_ref[...])


def _ffn_ln_kernel(x_ref, wg_ref, wu_ref, wd_ref, g_ref, b_ref, o_ref, *, alpha, ff_chunk):
    o_ref[...] = _swiglu_ln(x_ref[...], wg_ref, wu_ref, wd_ref, g_ref, b_ref, alpha, ff_chunk)


def _out_ffn_ln_kernel(x_ref, ocat_ref, yrw_ref, wuv_ref, woa_ref, wob_ref, g2_ref, b2_ref,
                       wg_ref, wu_ref, wd_ref, g3_ref, b3_ref, o_ref, *, alpha, ff_chunk):
    y_mla = _dot(ocat_ref[...], wuv_ref[...]).astype(BF16)
    y = _dot(y_mla, woa_ref[...]) + _dot(yrw_ref[...], wob_ref[...])
    x2 = _layernorm(alpha * x_ref[...] + y, g2_ref[...], b2_ref[...])
    o_ref[...] = _swiglu_ln(x2, wg_ref, wu_ref, wd_ref, g3_ref, b3_ref, alpha, ff_chunk)


def _row_tile(n, want):
    t = min(n, want)
    assert n % t == 0
    return t


def _ffn_chunk(d_ff):
    for c in (512, 256, 128):
        if d_ff % c == 0:
            return c
    return d_ff


def ffn_ln(x, wg, wu, wd, g, b, *, alpha, tm=512):
    n, d = x.shape
    tm = _row_tile(n, tm)
    kern = functools.partial(_ffn_ln_kernel, alpha=alpha, ff_chunk=_ffn_chunk(wg.shape[1]))
    row = pl.BlockSpec((tm, d), lambda i: (i, 0))
    return pl.pallas_call(
        kern, grid=(n // tm,),
        in_specs=[row, _const_spec(wg.shape), _const_spec(wu.shape), _const_spec(wd.shape),
                  _const_spec(g.shape), _const_spec(b.shape)],
        out_specs=row, out_shape=jax.ShapeDtypeStruct((n, d), F32),
        compiler_params=pltpu.CompilerParams(dimension_semantics=("arbitrary",), vmem_limit_bytes=VMEM_LIMIT),
        name="ffn_ln",
    )(x, wg, wu, wd, g, b)


def out_ffn_ln(x, ocat, yrw, wuv, woa, wob, g2, b2, wg, wu, wd, g3, b3, *, alpha, tm=512):
    n, d = x.shape
    tm = _row_tile(n, tm)
    kern = functools.partial(_out_ffn_ln_kernel, alpha=alpha, ff_chunk=_ffn_chunk(wg.shape[1]))
    row = lambda w: pl.BlockSpec((tm, w), lambda i: (i, 0))
    consts = (wuv, woa, wob, g2, b2, wg, wu, wd, g3, b3)
    return pl.pallas_call(
        kern, grid=(n // tm,),
        in_specs=[row(d), row(ocat.shape[1]), row(yrw.shape[1])] + [_const_spec(c.shape) for c in consts],
        out_specs=row(d), out_shape=jax.ShapeDtypeStruct((n, d), F32),
        compiler_params=pltpu.CompilerParams(dimension_semantics=("arbitrary",), vmem_limit_bytes=VMEM_LIMIT),
        name="out_ffn_ln",
    )(x, ocat, yrw, *consts)


def _mixer_in_kernel(x_ref, sb_ref, wmla_ref, wrw_ref, qg_ref, kvg_ref, wuqn_ref, wuqr_ref,
                     wuk_ref, invf_ref, mu_ref, w0_ref, wl_ref, a0_ref, al_ref, gl_ref, kk_ref, ka_ref,
                     ones2_ref,
                     q_out, kvcat_out, ckv_out, kr_out, r_out, lw_out, k_out, v_out, kkn_out, as_out, g_out,
                     sh_out, rw_sc, cos_sc, sin_sc, *, seq, tm, pos0, n_heads, q_scale, rw_dim, rope_dim):
    t = pl.program_id(1)
    xb = x_ref[...].astype(BF16)
    whole_seqs = tm >= seq

    row = lax.broadcasted_iota(jnp.int32, (tm, 1), 0)
    table_rows = slice(None) if whole_seqs else pl.ds(pl.multiple_of(t * tm, tm), tm)

    @pl.when(pl.program_id(0) == 0)
    def _():
        row_in_seq = (row % seq) if whole_seqs else (row + t * tm)
        ang = (row_in_seq.astype(F32) + pos0) * invf_ref[...]
        lane = lax.broadcasted_iota(jnp.int32, (tm, LANES), 1)
        cos_sc[table_rows, :] = jnp.where(lane < rope_dim, jnp.cos(ang), 0.0)
        sin_sc[table_rows, :] = jnp.where(lane < rope_dim, jnp.sin(ang), 0.0)

    cosm = cos_sc[table_rows, :]
    sinm = sin_sc[table_rows, :]

    def rope(grp):
        return grp * cosm + pltpu.roll(grp, LANES - rope_dim, axis=1) * sinm

    mla = _dot(xb, wmla_ref[...])
    q_rank = qg_ref.shape[1]
    ckv = _rmsnorm(mla[:, q_rank:q_rank + LANES], kvg_ref[...])
    kr = rope(mla[:, q_rank + LANES:])
    ckv_out[...] = ckv
    kr_out[...] = kr[:, :rope_dim]
    kvcat_out[:, :LANES] = ckv.astype(BF16)
    kvcat_out[:, LANES:] = kr.astype(BF16)

    cq = _rmsnorm(mla[:, :q_rank], qg_ref[...]).astype(BF16)
    q_nope = _dot(cq, wuqn_ref[...]).astype(BF16)
    q_rg = _dot(cq, wuqr_ref[...])
    nbg, rows = q_out.shape[0], q_out.shape[2]
    for hp in range(n_heads // HEAD_PAIR):
        q_lat = _dot(q_nope[:, hp * LANES:(hp + 1) * LANES], wuk_ref[hp])
        for j in range(HEAD_PAIR):
            h = hp * HEAD_PAIR + j
            q_h = jnp.concatenate([q_lat[:, j * LANES:(j + 1) * LANES] * q_scale,
                                   rope(q_rg[:, h * LANES:(h + 1) * LANES]) * q_scale], axis=-1).astype(BF16)
            q_out[:, h] = q_h.reshape(nbg, rows, 2 * LANES)

    rw = _dot(xb, wrw_ref[...])
    rw_sc[8:tm + 8, :] = rw
    if whole_seqs:
        nb = tm // seq
        rw_sc[7:8, :] = jnp.zeros((1, rw.shape[1]), F32)
        prev = rw_sc[7:tm + 7, :]
        first = jnp.broadcast_to(sb_ref[...], (nb, seq, rw.shape[1])).reshape(tm, rw.shape[1])
        prev = jnp.where(row % seq == 0, first, prev)
        sh_out[...] = rw.reshape(nb, seq, rw.shape[1])[:, seq - 1:seq, :]
    else:
        @pl.when(t == 0)
        def _():
            rw_sc[7:8, :] = sb_ref[0]
        prev = rw_sc[7:tm + 7, :]
        rw_sc[7:8, :] = rw[tm - 1:tm, :]
        sh_out[0] = rw[tm - 1:tm, :]
    rws = rw + (prev - rw) * mu_ref[...]

    r = rws[:, :rw_dim]
    k = rws[:, rw_dim:2 * rw_dim]
    v = rws[:, 2 * rw_dim:3 * rw_dim]
    lora_in = rws[:, 3 * rw_dim:3 * rw_dim + LANES]
    dg = rws[:, 3 * rw_dim + LANES:]
    z = w0_ref[...] + _dot(jnp.tanh(lora_in).astype(BF16), wl_ref[...])
    nz = -z
    softplus = jnp.maximum(nz, 0.0) + jnp.log(1.0 + jnp.exp(-jnp.abs(nz)))
    w = -softplus - 0.5
    a = _sigmoid(a0_ref[...] + _dot(lora_in.astype(BF16), al_ref[...]))
    g = _dot(_sigmoid(dg).astype(BF16), gl_ref[...])
    kk = k * kk_ref[...]
    norm = jnp.sqrt(_head_sums(kk * kk, ones2_ref[...]))
    kk = kk / jnp.maximum(norm, 1e-12)
    r_out[...] = r
    lw_out[...] = -jnp.exp(w)
    k_out[...] = k * (1.0 + (a - 1.0) * ka_ref[...])
    v_out[...] = v
    kkn_out[...] = kk
    as_out[...] = a
    g_out[...] = g


def mixer_in(x, shift_buf, wts, *, n_seq, seq, pos0, tm, n_heads, q_scale, rw_dim, rope_dim):
    n, d = x.shape
    rw_cols = wts["wrw"].shape[1]
    if tm >= seq:
        assert tm % seq == 0 and n % tm == 0
        nb = tm // seq
        grid = (n // tm, 1)
        rows = seq
        nbg = nb
        row_idx = lambda i, t: (i, 0)
        sb_spec = pl.BlockSpec((nb, 1, rw_cols), lambda i, t: (i, 0, 0))
        q_spec = pl.BlockSpec((nb, n_heads, seq, 2 * LANES), lambda i, t: (i, 0, 0, 0))
        q_shape = (n_seq, n_heads, seq, 2 * LANES)
    else:
        assert seq % tm == 0
        tps = seq // tm
        grid = (n_seq, tps)
        rows = tm
        nbg = 1
        row_idx = lambda i, t: (i * tps + t, 0)
        sb_spec = pl.BlockSpec((1, 1, rw_cols), lambda i, t: (i, 0, 0))
        q_spec = pl.BlockSpec((1, n_heads, tm, 2 * LANES), lambda i, t: (i * tps + t, 0, 0, 0))
        q_shape = (n_seq * tps, n_heads, tm, 2 * LANES)
    rowspec = lambda w: pl.BlockSpec((tm, w), row_idx)
    names = ("wmla", "wrw", "qg", "kvg", "wuqn", "wuqr", "wuk", "invf", "mu", "w0", "wl", "a0",
             "al", "gl", "kk", "ka", "ones2")
    consts = [wts[k] for k in names]
    kern = functools.partial(_mixer_in_kernel, seq=seq, tm=tm, pos0=float(pos0), n_heads=n_heads,
                             q_scale=q_scale, rw_dim=rw_dim, rope_dim=rope_dim)
    f = lambda w, dt=F32: jax.ShapeDtypeStruct((n, w), dt)
    out_shape = (jax.ShapeDtypeStruct(q_shape, BF16), f(2 * LANES, BF16), f(LANES), f(rope_dim)) \
        + (f(rw_dim),) * 7 + (jax.ShapeDtypeStruct((n_seq, 1, rw_cols), F32),)
    out_specs = (q_spec, rowspec(2 * LANES), rowspec(LANES), rowspec(rope_dim)) + (rowspec(rw_dim),) * 7 + (sb_spec,)
    return pl.pallas_call(
        kern, grid=grid,
        in_specs=[rowspec(d), sb_spec] + [_const_spec(c.shape) for c in consts],
        out_specs=out_specs, out_shape=out_shape,
        scratch_shapes=[pltpu.VMEM((tm + 8, rw_cols), F32)] + [pltpu.VMEM((max(tm, seq), LANES), F32)] * 2,
        compiler_params=pltpu.CompilerParams(dimension_semantics=("arbitrary", "arbitrary"),
                                             vmem_limit_bytes=VMEM_LIMIT),
        name="mixer_in",
    )(x, shift_buf, *consts)


def _prompt_attn_kernel(q_ref, kv_ref, o_ref, m_sc, acc_sc, s_sc, *, tq, tk, n_heads):
    qb = pl.program_id(1)
    q = q_ref[0]
    m_sc[...] = jnp.full(m_sc.shape, NEG, F32)
    acc_sc[...] = jnp.zeros(acc_sc.shape, F32)
    ones = jnp.ones((tk, LANES), BF16)

    def scores(start, masked):
        s = _dot_nt(q, kv_ref[0, pl.ds(start, tk), :])
        if masked:
            qi = lax.broadcasted_iota(jnp.int32, s.shape, 0) % tq + qb * tq
            kj = lax.broadcasted_iota(jnp.int32, s.shape, 1) + start
            s = jnp.where(qi >= kj, s, NEG)
        return s

    def consume(s, start):
        m_prev = m_sc[...]
        m_new = jnp.maximum(m_prev, jnp.max(s, axis=-1, keepdims=True))
        corr = jnp.exp2(m_prev - m_new)
        p = jnp.exp2(s - jnp.tile(m_new, (1, tk // LANES))).astype(BF16)
        v_ext = jnp.concatenate([kv_ref[0, pl.ds(start, tk), :LANES], ones], axis=-1)
        acc_sc[...] = acc_sc[...] * jnp.tile(corr, (1, 2)) + _dot(p, v_ext)
        m_sc[...] = m_new

    n_full = (qb * tq) // tk
    blk = lambda j: pl.multiple_of(j * tk, tk)

    @pl.when(n_full == 0)
    def _():
        s_sc[...] = scores(0, True)

    @pl.when(n_full > 0)
    def _():
        s_sc[...] = scores(0, False)

    def body(j, carry):
        s = s_sc[...]
        s_next = scores(blk(j + 1), False)
        consume(s, blk(j))
        s_sc[...] = s_next
        return carry

    lax.fori_loop(0, n_full - 1, body, 0)

    @pl.when(n_full > 0)
    def _():
        s = s_sc[...]
        s_next = scores(blk(n_full), True)
        consume(s, blk(n_full - 1))
        s_sc[...] = s_next

    consume(s_sc[...], blk(n_full))
    acc = acc_sc[...]
    o = (acc[:, :LANES] / acc[:, LANES:]).astype(BF16)
    for h in range(n_heads):
        o_ref[0, :, h * LANES:(h + 1) * LANES] = o[h * tq:(h + 1) * tq]


def prompt_attn(q, kvcat, *, n_seq, seq, tq, n_heads):
    nq = seq // tq
    m = n_heads * tq
    tk = 2 * tq if seq % (2 * tq) == 0 else tq
    kern = functools.partial(_prompt_attn_kernel, tq=tq, tk=tk, n_heads=n_heads)
    return pl.pallas_call(
        kern, grid=(n_seq, nq),
        in_specs=[pl.BlockSpec((1, m, 2 * LANES), lambda b, i: (b * nq + i, 0, 0)),
                  pl.BlockSpec((1, seq, 2 * LANES), lambda b, i: (b, 0, 0))],
        out_specs=pl.BlockSpec((1, tq, n_heads * LANES), lambda b, i: (b, i, 0)),
        out_shape=jax.ShapeDtypeStruct((n_seq, seq, n_heads * LANES), BF16),
        scratch_shapes=[pltpu.VMEM((m, LANES), F32), pltpu.VMEM((m, 2 * LANES), F32), pltpu.VMEM((m, tk), F32)],
        compiler_params=pltpu.CompilerParams(dimension_semantics=("arbitrary", "arbitrary"),
                                             vmem_limit_bytes=VMEM_LIMIT),
        name="prompt_attn",
    )(q, kvcat)


def _softmax_update(s, vals, m_sc, l_sc, acc_sc):
    m_prev = m_sc[...]
    m_new = jnp.maximum(m_prev, jnp.max(s, axis=-1, keepdims=True))
    corr = jnp.exp2(m_prev - m_new)
    p = jnp.exp2(s - m_new)
    l_sc[...] = l_sc[...] * corr + jnp.sum(p, axis=-1, keepdims=True)
    acc_sc[...] = acc_sc[...] * corr + _dot(p.astype(BF16), vals)
    m_sc[...] = m_new


SEQS_PER_STEP = 2


def _sample_attn_kernel(pt_ref, q_ref, kvnew_ref, ckv_hbm, kr_hbm, o_ref, ckv_buf, kr_buf, sem, m_sc, l_sc, acc_sc,
                        s_sc, *, n_pages, tile_pages, rope_dim, dec_seq, n_heads, n_seq):
    i = pl.program_id(0)
    tp = tile_pages
    n_tiles = n_pages // tp

    def page_copies(page, slot, p):
        return (pltpu.make_async_copy(ckv_hbm.at[page], ckv_buf.at[slot, p], sem.at[slot, 0]),
                pltpu.make_async_copy(kr_hbm.at[page], kr_buf.at[slot, p], sem.at[slot, 1]))

    def start_gather(seq, slot):
        def body(p, carry):
            for cp in page_copies(pt_ref[seq * n_pages + p], slot, p):
                cp.start()
            return carry
        lax.fori_loop(0, n_pages, body, 0, unroll=8)

    def wait_gather(slot):
        for p in range(n_pages):
            for cp in page_copies(0, slot, p):
                cp.wait()

    def attend(j, slot):
        q = q_ref[j]
        q_lat, q_rope = q[:, :LANES], q[:, LANES:LANES + rope_dim]
        m_sc[...] = jnp.full(m_sc.shape, NEG, F32)
        l_sc[...] = jnp.zeros(l_sc.shape, F32)
        acc_sc[...] = jnp.zeros(acc_sc.shape, F32)

        def vals(t):
            c = ckv_buf[slot, pl.ds(t * tp, tp)]
            return c.reshape(tp * LANES, LANES).astype(BF16)

        def scores(t):
            kr = kr_buf[slot, pl.ds(t * tp, tp)].astype(BF16)
            kr_cat = jnp.concatenate([kr[u] for u in range(tp)], axis=-1)
            return _dot_nt(q_lat, vals(t)) + _dot(q_rope, kr_cat)

        s_sc[...] = scores(0)

        def body(t, carry):
            s = s_sc[...]
            s_next = scores(t + 1)
            _softmax_update(s, vals(t), m_sc, l_sc, acc_sc)
            s_sc[...] = s_next
            return carry

        lax.fori_loop(0, n_tiles - 1, body, 0)
        _softmax_update(s_sc[...], vals(n_tiles - 1), m_sc, l_sc, acc_sc)
        kn = kvnew_ref[j]
        s = _dot_nt(q, kn)
        qi = lax.broadcasted_iota(jnp.int32, s.shape, 0) % dec_seq
        kj = lax.broadcasted_iota(jnp.int32, s.shape, 1)
        _softmax_update(jnp.where(qi >= kj, s, NEG), kn[:, :LANES], m_sc, l_sc, acc_sc)
        o = (acc_sc[...] / l_sc[...]).astype(BF16)
        for h in range(n_heads):
            o_ref[j, :, h * LANES:(h + 1) * LANES] = o[h * dec_seq:(h + 1) * dec_seq]

    first = SEQS_PER_STEP * i

    @pl.when(i == 0)
    def _():
        start_gather(0, 0)

    start_gather(first + 1, 1)
    wait_gather(0)
    attend(0, 0)

    @pl.when(first + 2 < n_seq)
    def _():
        start_gather(first + 2, 0)

    wait_gather(1)
    attend(1, 1)


def sample_attn(q, kvnew, cache_ckv, cache_kr_t, page_table, *, dec_seq, n_heads, tile_pages):
    b, n_pages = page_table.shape
    page = cache_ckv.shape[1]
    rope_dim = cache_kr_t.shape[1]
    assert n_pages % tile_pages == 0 and page == LANES and b % SEQS_PER_STEP == 0
    m = n_heads * dec_seq
    kern = functools.partial(_sample_attn_kernel, n_pages=n_pages, tile_pages=tile_pages, rope_dim=rope_dim,
                             dec_seq=dec_seq, n_heads=n_heads, n_seq=b)
    per_step = lambda rows, width: pl.BlockSpec((SEQS_PER_STEP, rows, width), lambda i, pt: (i, 0, 0))
    grid_spec = pltpu.PrefetchScalarGridSpec(
        num_scalar_prefetch=1, grid=(b // SEQS_PER_STEP,),
        in_specs=[per_step(m, 2 * LANES), per_step(dec_seq, 2 * LANES),
                  pl.BlockSpec(memory_space=pl.ANY), pl.BlockSpec(memory_space=pl.ANY)],
        out_specs=per_step(dec_seq, n_heads * LANES),
        scratch_shapes=[pltpu.VMEM((SEQS_PER_STEP, n_pages, page, LANES), F32),
                        pltpu.VMEM((SEQS_PER_STEP, n_pages, rope_dim, page), F32),
                        pltpu.SemaphoreType.DMA((SEQS_PER_STEP, 2)),
                        pltpu.VMEM((m, 1), F32), pltpu.VMEM((m, 1), F32), pltpu.VMEM((m, LANES), F32),
                        pltpu.VMEM((m, tile_pages * page), F32)])
    return pl.pallas_call(
        kern, grid_spec=grid_spec,
        out_shape=jax.ShapeDtypeStruct((b, dec_seq, n_heads * LANES), BF16),
        compiler_params=pltpu.CompilerParams(dimension_semantics=("arbitrary",), vmem_limit_bytes=VMEM_LIMIT),
        name="sample_attn",
    )(page_table.reshape(-1), q, kvnew, cache_ckv, cache_kr_t)


def _cumsum_rows(tri, x):
    if x.shape[0] < 16:
        return jnp.dot(tri.astype(F32), x, preferred_element_type=F32, precision=lax.Precision.HIGHEST)
    hi = x.astype(BF16)
    lo = (x - hi.astype(F32)).astype(BF16)
    return _dot(tri, hi) + _dot(tri, lo)


def _rwkv_chunk(rs, lws, ks, vs, kks, asigs, s0s, tri, strict, incl, head0):
    n = range(len(rs))
    c = rs[0].shape[0]
    c2 = 2 * c

    def stack(x):
        return jnp.concatenate([jnp.where(head0, x, 0.0), jnp.where(head0, 0.0, x)], axis=0)

    cs = [_cumsum_rows(tri, lws[i]) for i in n]
    cs_end = [cs[i][c - 1:c, :] for i in n]
    e_neg = [jnp.exp(-cs[i]) for i in n]
    e_end = [jnp.exp(cs_end[i] - cs[i]) for i in n]
    bvec = [kks[i] * asigs[i] for i in n]
    a_t = [stack(-kks[i] * jnp.exp(cs[i] - lws[i])) for i in n]
    r_t = [stack(rs[i] * jnp.exp(cs[i])).astype(BF16) for i in n]
    k_t = [stack(ks[i] * e_neg[i]) for i in n]
    b_t = [stack(bvec[i] * e_neg[i]) for i in n]
    k_e = [stack(ks[i] * e_end[i]).astype(BF16) for i in n]
    b_e = [stack(bvec[i] * e_end[i]).astype(BF16) for i in n]
    v_s = [stack(vs[i]).astype(BF16) for i in n]

    sc = [_dot_nt(jnp.concatenate([a_t[i].astype(BF16), r_t[i]], axis=0),
                  jnp.concatenate([k_t[i], b_t[i]], axis=0).astype(BF16)) for i in n]
    l_ak = [jnp.where(strict, sc[i][:c2, :c2], 0.0).astype(BF16) for i in n]
    lp = [jnp.where(strict, sc[i][:c2, c2:], 0.0).astype(BF16) for i in n]
    a_rk = [jnp.where(incl, sc[i][c2:, :c2], 0.0).astype(BF16) for i in n]
    a_rb = [jnp.where(incl, sc[i][c2:, c2:], 0.0).astype(BF16) for i in n]

    x = [jnp.concatenate([a_t[i], _dot(l_ak[i], v_s[i])], axis=-1) for i in n]
    n_fac = int(math.log2(c))
    for f in range(n_fac):
        x = [x[i] + _dot(lp[i], x[i].astype(BF16)) for i in n]
        if f + 1 < n_fac:
            lp = [_dot(lp[i], lp[i]).astype(BF16) for i in n]

    s0b = [s0s[i].astype(BF16) for i in n]
    u = [(_dot_nt(x[i][:, :LANES].astype(BF16), s0b[i]) + x[i][:, LANES:]).astype(BF16) for i in n]
    y_st = [_dot_nt(r_t[i], s0b[i]) + _dot(a_rk[i], v_s[i]) + _dot(a_rb[i], u[i]) for i in n]
    ys = [y_st[i][:c] + y_st[i][c:] for i in n]
    s_new = [s0s[i] * jnp.exp(cs_end[i]) + _dot_tn(v_s[i], k_e[i]) + _dot_tn(u[i], b_e[i]) for i in n]
    return ys, s_new


def _rwkv_kernel(r_ref, lw_ref, k_ref, v_ref, kk_ref, as_ref, g_ref, s0_ref, rk_ref, lg_ref, lb_ref, ones2_ref,
                 y_ref, sout_ref, s_sc, *, chunk, n_pairs, n_seq_blk):
    ci = pl.program_id(1)

    @pl.when(ci == 0)
    def _():
        s_sc[...] = s0_ref[...]

    c = chunk
    ti = lax.broadcasted_iota(jnp.int32, (c, c), 0)
    tj = lax.broadcasted_iota(jnp.int32, (c, c), 1)
    tri = (ti >= tj).astype(BF16)
    si = lax.broadcasted_iota(jnp.int32, (2 * c, 2 * c), 0)
    sj = lax.broadcasted_iota(jnp.int32, (2 * c, 2 * c), 1)
    strict = si > sj
    incl = si >= sj
    head0 = lax.broadcasted_iota(jnp.int32, (c, LANES), 1) < (LANES // HEAD_PAIR)
    ones2 = ones2_ref[...]
    inv_n = 1.0 / (LANES // HEAD_PAIR)

    streams = [(b, p) for b in range(n_seq_blk) for p in range(n_pairs)]
    n = range(len(streams))
    lanes = [slice(p * LANES, (p + 1) * LANES) for _, p in streams]
    get = lambda ref: [ref[b, :, lanes[i]] for i, (b, _) in enumerate(streams)]
    par = lambda ref: [ref[:, lanes[i]] for i in n]
    rs, ks, vs = get(r_ref), get(k_ref), get(v_ref)
    ys, s_new = _rwkv_chunk(rs, get(lw_ref), ks, vs, get(kk_ref), get(as_ref),
                            [s_sc[b, p] for b, p in streams], tri, strict, incl, head0)
    for i, (b, p) in enumerate(streams):
        s_sc[b, p] = s_new[i]
    lg, lb, rk = par(lg_ref), par(lb_ref), par(rk_ref)
    mu = [_head_sums(ys[i], ones2) * inv_n for i in n]
    d = [ys[i] - mu[i] for i in n]
    var = [_head_sums(d[i] * d[i], ones2) * inv_n for i in n]
    bonus = [_head_sums(rs[i] * ks[i] * rk[i], ones2) * vs[i] for i in n]
    gs = get(g_ref)
    for i, (b, p) in enumerate(streams):
        o = d[i] * lax.rsqrt(var[i] + GN_EPS) * lg[i] + lb[i] + bonus[i]
        y_ref[b, :, lanes[i]] = (o * gs[i]).astype(BF16)

    @pl.when(ci == pl.num_programs(1) - 1)
    def _():
        sout_ref[...] = s_sc[...]


def rwkv_chunked(r, lw, k, v, kk, asig, g, s0, rk, lg, lb, ones2, *, chunk, n_seq_blk):
    n_seq, seq, rw_dim = r.shape
    n_pairs = rw_dim // LANES
    nc = seq // chunk
    assert n_seq % n_seq_blk == 0 and seq % chunk == 0
    kern = functools.partial(_rwkv_kernel, chunk=chunk, n_pairs=n_pairs, n_seq_blk=n_seq_blk)
    row = pl.BlockSpec((n_seq_blk, chunk, rw_dim), lambda b, c: (b, c, 0))
    st = pl.BlockSpec((n_seq_blk, n_pairs, LANES, LANES), lambda b, c: (b, 0, 0, 0))
    return pl.pallas_call(
        kern, grid=(n_seq // n_seq_blk, nc),
        in_specs=[row] * 7 + [st] + [_const_spec(x.shape) for x in (rk, lg, lb, ones2)],
        out_specs=(row, st),
        out_shape=(jax.ShapeDtypeStruct((n_seq, seq, rw_dim), BF16), jax.ShapeDtypeStruct(s0.shape, F32)),
        scratch_shapes=[pltpu.VMEM((n_seq_blk, n_pairs, LANES, LANES), F32)],
        compiler_params=pltpu.CompilerParams(dimension_semantics=("arbitrary", "arbitrary"),
                                             vmem_limit_bytes=VMEM_LIMIT),
        name="rwkv_chunked",
    )(r, lw, k, v, kk, asig, g, s0, rk, lg, lb, ones2)


def _block_diag(blocks):
    n, r, c = blocks.shape
    eye = jnp.eye(n, dtype=blocks.dtype)
    return (eye[:, None, :, None] * blocks[:, :, None, :]).reshape(n * r, n * c)


def _rope_group(w, rope_dim):
    half = rope_dim // 2
    x1, x2 = w[..., :half], w[..., half:]
    pad = jnp.zeros(w.shape[:-1] + (LANES - 2 * rope_dim,), w.dtype)
    return jnp.concatenate([x1, x2, -x2, x1, pad], axis=-1)


def _prep_weights(p, dims):
    q_rank, kv_rank, rope_dim, n_heads, nope, rw_dim = (dims[k] for k in
                                                        ("q_rank", "kv_rank", "rope_dim", "n_heads", "nope", "rw_dim"))
    w_in = p["w_in"]
    o1, o2 = q_rank + kv_rank, q_rank + kv_rank + rope_dim
    w_uq = p["w_uq"]
    row = lambda a: a.reshape(1, -1).astype(F32)
    d_wl = p["w_lora_up"].shape[0]
    d_al = p["a_lora_up"].shape[0]
    assert d_wl + d_al == LANES
    half = jnp.arange(0, rope_dim, 2, dtype=F32)
    inv = ROPE_BASE ** (-half / rope_dim)
    invf = jnp.concatenate([inv, inv, jnp.zeros((LANES - rope_dim,), F32)]).reshape(1, LANES)
    rw_head = rw_dim // dims["rw_heads"]
    assert HEAD_PAIR * rw_head == LANES and HEAD_PAIR * nope == LANES and kv_rank == LANES
    pair_ones = _block_diag(jnp.ones((HEAD_PAIR, rw_head, rw_head), F32))
    w_out = p["w_out"]
    d_mla = n_heads * dims["mla_v"]
    wuk = jnp.transpose(p["w_uk"], (1, 2, 0)).reshape(n_heads // HEAD_PAIR, HEAD_PAIR, nope, kv_rank)
    return {
        "wmla": jnp.concatenate([w_in[:, :o1], _rope_group(w_in[:, o1:o2], rope_dim)], axis=1).astype(BF16),
        "wrw": w_in[:, o2:].astype(BF16),
        "qg": row(p["q_norm_g"]), "kvg": row(p["kv_norm_g"]),
        "wuqn": w_uq[:, :, :nope].reshape(q_rank, n_heads * nope).astype(BF16),
        "wuqr": _rope_group(w_uq[:, :, nope:], rope_dim).reshape(q_rank, n_heads * LANES).astype(BF16),
        "wuk": jnp.stack([_block_diag(w) for w in wuk]).astype(BF16),
        "invf": invf,
        "mu": row(p["shift_mu"]), "w0": row(p["w0"]), "a0": row(p["a0"]),
        "wl": jnp.concatenate([p["w_lora_up"], jnp.zeros((d_al, rw_dim), F32)], axis=0).astype(BF16),
        "al": jnp.concatenate([jnp.zeros((d_wl, rw_dim), F32), p["a_lora_up"]], axis=0).astype(BF16),
        "gl": p["g_lora_up"].astype(BF16),
        "kk": row(p["k_k"]), "ka": row(p["k_a"]),
        "ones2": jnp.concatenate([pair_ones, pair_ones], axis=0).astype(BF16),
        "rk": row(p["r_k"]), "lg": row(p["lnx_g"]), "lb": row(p["lnx_b"]),
        "wuv": _block_diag(jnp.transpose(p["w_uv"], (1, 0, 2))).astype(BF16),
        "woa": w_out[:d_mla].astype(BF16), "wob": w_out[d_mla:].astype(BF16),
    }


def _pair_state(s):
    b, h, n, _ = s.shape
    s = s.reshape(b, h // HEAD_PAIR, HEAD_PAIR, n, n)
    eye = jnp.eye(HEAD_PAIR, dtype=s.dtype)
    return (s[:, :, :, :, None, :] * eye[None, None, :, None, :, None]).reshape(b, h // HEAD_PAIR, HEAD_PAIR * n,
                                                                                 HEAD_PAIR * n)


def _unpair_state(s, n):
    b, hp = s.shape[:2]
    s = s.reshape(b, hp, HEAD_PAIR, n, HEAD_PAIR, n)
    return jnp.stack([s[:, :, i, :, i, :] for i in range(HEAD_PAIR)], axis=2).reshape(b, hp * HEAD_PAIR, n, n)


def _group(x, shift_buf, wkv0, wts, ffa, ffb, lns, dims, *, pos0, attend, tm_in, chunk, rw_seq_blk, alpha):
    n_seq, seq, d = x.shape
    n = n_seq * seq
    n_heads, rw_dim, rope_dim = dims["n_heads"], dims["rw_dim"], dims["rope_dim"]
    x1 = ffn_ln(x.reshape(n, d), *ffa, lns["ln1_g"], lns["ln1_b"], alpha=alpha)
    (q, kvcat, ckv, kr, r, lw, k, v, kk, asig, g, shift_out) = mixer_in(
        x1, shift_buf.reshape(n_seq, 1, -1), wts, n_seq=n_seq, seq=seq, pos0=pos0, tm=tm_in, n_heads=n_heads,
        q_scale=dims["q_scale"], rw_dim=rw_dim, rope_dim=rope_dim)
    ocat = attend(q.reshape(q.shape[0], n_heads * q.shape[2], q.shape[3]), kvcat.reshape(n_seq, seq, -1))
    seqs = lambda a: a.reshape(n_seq, seq, -1)
    yrw, s_out = rwkv_chunked(seqs(r), seqs(lw), seqs(k), seqs(v), seqs(kk), seqs(asig), seqs(g), _pair_state(wkv0),
                              wts["rk"], wts["lg"], wts["lb"], wts["ones2"], chunk=chunk, n_seq_blk=rw_seq_blk)
    y = out_ffn_ln(x1, ocat.reshape(n, -1), yrw.reshape(n, -1), wts["wuv"], wts["woa"], wts["wob"], lns["ln2_g"], lns["ln2_b"],
                   *ffb, lns["ln3_g"], lns["ln3_b"], alpha=alpha)
    return (y.reshape(n_seq, seq, d), ckv.reshape(1, n_seq, seq, -1), kr.reshape(1, n_seq, seq, -1),
            _unpair_state(s_out, rw_dim // dims["rw_heads"])[None], shift_out.reshape(1, n_seq, -1))


def kernel(x_prompt, x_sample, cache_ckv, cache_krope, state_wkv, state_shift, page_table, ln1_g, ln1_b, ffa_w_gate, ffa_w_up, ffa_w_down, w_in, q_norm_g, w_uq, kv_norm_g, w_uk, w_uv, shift_mu, w0, w_lora_up, a0, a_lora_up, g_lora_up, k_k, k_a, r_k, lnx_g, lnx_b, w_out, ln2_g, ln2_b, ffb_w_gate, ffb_w_up, ffb_w_down, ln3_g, ln3_b):
    depth = w_in.shape[0]
    assert depth == 1
    alpha = (2 * depth) ** 0.25
    n_heads, nope_rope = w_uq.shape[2], w_uq.shape[3]
    rope_dim = cache_krope.shape[-1]
    rw_heads, rw_head = state_wkv.shape[2], state_wkv.shape[3]
    dims = dict(q_rank=w_uq.shape[1], kv_rank=w_uk.shape[1], rope_dim=rope_dim, n_heads=n_heads,
                nope=nope_rope - rope_dim, rw_dim=rw_heads * rw_head, rw_heads=rw_heads, mla_v=w_uv.shape[3],
                q_scale=float(nope_rope) ** -0.5 * math.log2(math.e))
    p = dict(w_in=w_in[0], q_norm_g=q_norm_g[0], w_uq=w_uq[0], kv_norm_g=kv_norm_g[0], w_uk=w_uk[0], w_uv=w_uv[0],
             shift_mu=shift_mu[0], w0=w0[0], w_lora_up=w_lora_up[0], a0=a0[0], a_lora_up=a_lora_up[0],
             g_lora_up=g_lora_up[0], k_k=k_k[0], k_a=k_a[0], r_k=r_k[0], lnx_g=lnx_g[0], lnx_b=lnx_b[0],
             w_out=w_out[0])
    wts = _prep_weights(p, dims)
    row = lambda a: a[0].reshape(1, -1)
    lns = dict(ln1_g=row(ln1_g), ln1_b=row(ln1_b), ln2_g=row(ln2_g), ln2_b=row(ln2_b), ln3_g=row(ln3_g),
               ln3_b=row(ln3_b))
    ffa = (ffa_w_gate[0].astype(BF16), ffa_w_up[0].astype(BF16), ffa_w_down[0].astype(BF16))
    ffb = (ffb_w_gate[0].astype(BF16), ffb_w_up[0].astype(BF16), ffb_w_down[0].astype(BF16))

    bp, sp, _ = x_prompt.shape
    bs, ss, _ = x_sample.shape
    n_pages, page = page_table.shape[1], cache_ckv.shape[2]
    past_len = n_pages * page
    tq = min(256, sp)

    attend_p = functools.partial(prompt_attn, n_seq=bp, seq=sp, tq=tq, n_heads=n_heads)
    out_p = _group(x_prompt, jnp.zeros((bp, state_shift.shape[-1]), F32),
                   jnp.zeros((bp,) + state_wkv.shape[2:], F32), wts, ffa, ffb, lns, dims,
                   pos0=0.0, attend=attend_p, tm_in=tq, chunk=min(64, sp), rw_seq_blk=math.gcd(bp, 4), alpha=alpha)

    kr_t = jnp.swapaxes(cache_krope[0], 1, 2)
    attend_s = lambda q, kvnew: sample_attn(q, kvnew, cache_ckv[0], kr_t, page_table, dec_seq=ss,
                                            n_heads=n_heads, tile_pages=math.gcd(n_pages, 32))
    out_s = _group(x_sample, state_shift[0], state_wkv[0], wts, ffa, ffb, lns, dims,
                   pos0=float(past_len), attend=attend_s, tm_in=min(128, bs * ss), chunk=ss,
                   rw_seq_blk=math.gcd(bs, 8), alpha=alpha)
    return (out_p[0], out_s[0]) + out_p[1:] + out_s[1:]
```

```python
import functools
import math

import jax
import jax.numpy as jnp
from jax import lax
from jax.experimental import pallas as pl
from jax.experimental.pallas import tpu as pltpu

F32 = jnp.float32
BF16 = jnp.bfloat16

LANES = 128
HEAD_PAIR = 2
LN_EPS = 1e-5
RMS_EPS = 1e-6
GN_EPS = 64e-5
ROPE_BASE = 10000.0
NEG = -1e30
VMEM_LIMIT = 56 * 1024 * 1024


def _dot(a, b):
    return jnp.dot(a, b, preferred_element_type=F32)


def _dot_nt(a, b):
    return lax.dot_general(a, b, (((1,), (1,)), ((), ())), preferred_element_type=F32)


def _dot_tn(a, b):
    return lax.dot_general(a, b, (((0,), (0,)), ((), ())), preferred_element_type=F32)


def _layernorm(y, g, b):
    mu = jnp.mean(y, axis=-1, keepdims=True)
    d = y - mu
    var = jnp.mean(d * d, axis=-1, keepdims=True)
    return d * lax.rsqrt(var + LN_EPS) * g + b


def _rmsnorm(y, g):
    return y * lax.rsqrt(jnp.mean(y * y, axis=-1, keepdims=True) + RMS_EPS) * g


def _sigmoid(x):
    return 1.0 / (1.0 + jnp.exp(-x))


def _head_sums(x, ones2):
    hi = x.astype(BF16)
    lo = (x - hi.astype(F32)).astype(BF16)
    groups = [slice(g * LANES, (g + 1) * LANES) for g in range(x.shape[1] // LANES)]
    return jnp.concatenate([_dot(jnp.concatenate([hi[:, g], lo[:, g]], axis=-1), ones2) for g in groups], axis=-1)


def _const_spec(shape):
    nd = len(shape)
    return pl.BlockSpec(shape, lambda *_: (0,) * nd, pipeline_mode=pl.Buffered(1))


def _swiglu_ln(x, wg_ref, wu_ref, wd_ref, g_ref, b_ref, alpha, ff_chunk):
    xb = x.astype(BF16)
    d_ff = wg_ref.shape[1]
    acc = jnp.zeros(x.shape, F32)
    for c in range(d_ff // ff_chunk):
        sl = slice(c * ff_chunk, (c + 1) * ff_chunk)
        gate = _dot(xb, wg_ref[:, sl])
        up = _dot(xb, wu_ref[:, sl])
        h = (gate * _sigmoid(gate) * up).astype(BF16)
        acc = acc + _dot(h, wd_ref[sl, :])
    return _layernorm(alpha * x + 0.5 * acc, g_ref[...], b_ref[...])


def _ffn_ln_kernel(x_ref, wg_ref, wu_ref, wd_ref, g_ref, b_ref, o_ref, *, alpha, ff_chunk):
    o_ref[...] = _swiglu_ln(x_ref[...], wg_ref, wu_ref, wd_ref, g_ref, b_ref, alpha, ff_chunk)


def _out_ffn_ln_kernel(x_ref, ocat_ref, yrw_ref, wuv_ref, woa_ref, wob_ref, g2_ref, b2_ref,
                       wg_ref, wu_ref, wd_ref, g3_ref, b3_ref, o_ref, *, alpha, ff_chunk):
    y_mla = _dot(ocat_ref[...], wuv_ref[...]).astype(BF16)
    y = _dot(y_mla, woa_ref[...]) + _dot(yrw_ref[...], wob_ref[...])
    x2 = _layernorm(alpha * x_ref[...] + y, g2_ref[...], b2_ref[...])
    o_ref[...] = _swiglu_ln(x2, wg_ref, wu_ref, wd_ref, g3_ref, b3_ref, alpha, ff_chunk)


def _row_tile(n, want):
    t = min(n, want)
    assert n % t == 0
    return t


def _ffn_chunk(d_ff):
    for c in (512, 256, 128):
        if d_ff % c == 0:
            return c
    return d_ff


def ffn_ln(x, wg, wu, wd, g, b, *, alpha, tm=512):
    n, d = x.shape
    tm = _row_tile(n, tm)
    kern = functools.partial(_ffn_ln_kernel, alpha=alpha, ff_chunk=_ffn_chunk(wg.shape[1]))
    row = pl.BlockSpec((tm, d), lambda i: (i, 0))
    return pl.pallas_call(
        kern, grid=(n // tm,),
        in_specs=[row, _const_spec(wg.shape), _const_spec(wu.shape), _const_spec(wd.shape),
                  _const_spec(g.shape), _const_spec(b.shape)],
        out_specs=row, out_shape=jax.ShapeDtypeStruct((n, d), F32),
        compiler_params=pltpu.CompilerParams(dimension_semantics=("arbitrary",), vmem_limit_bytes=VMEM_LIMIT),
        name="ffn_ln",
    )(x, wg, wu, wd, g, b)


def out_ffn_ln(x, ocat, yrw, wuv, woa, wob, g2, b2, wg, wu, wd, g3, b3, *, alpha, tm=512):
    n, d = x.shape
    tm = _row_tile(n, tm)
    kern = functools.partial(_out_ffn_ln_kernel, alpha=alpha, ff_chunk=_ffn_chunk(wg.shape[1]))
    row = lambda w: pl.BlockSpec((tm, w), lambda i: (i, 0))
    consts = (wuv, woa, wob, g2, b2, wg, wu, wd, g3, b3)
    return pl.pallas_call(
        kern, grid=(n // tm,),
        in_specs=[row(d), row(ocat.shape[1]), row(yrw.shape[1])] + [_const_spec(c.shape) for c in consts],
        out_specs=row(d), out_shape=jax.ShapeDtypeStruct((n, d), F32),
        compiler_params=pltpu.CompilerParams(dimension_semantics=("arbitrary",), vmem_limit_bytes=VMEM_LIMIT),
        name="out_ffn_ln",
    )(x, ocat, yrw, *consts)


def _mixer_in_kernel(x_ref, sb_ref, wmla_ref, wrw_ref, qg_ref, kvg_ref, wuqn_ref, wuqr_ref,
                     wuk_ref, invf_ref, mu_ref, w0_ref, wl_ref, a0_ref, al_ref, gl_ref, kk_ref, ka_ref,
                     ones2_ref,
                     q_out, kvcat_out, ckv_out, kr_out, r_out, lw_out, k_out, v_out, kkn_out, as_out, g_out,
                     sh_out, rw_sc, cos_sc, sin_sc, *, seq, tm, pos0, n_heads, q_scale, rw_dim, rope_dim):
    t = pl.program_id(1)
    xb = x_ref[...].astype(BF16)
    whole_seqs = tm >= seq

    row = lax.broadcasted_iota(jnp.int32, (tm, 1), 0)
    table_rows = slice(None) if whole_seqs else pl.ds(pl.multiple_of(t * tm, tm), tm)

    @pl.when(pl.program_id(0) == 0)
    def _():
        row_in_seq = (row % seq) if whole_seqs else (row + t * tm)
        ang = (row_in_seq.astype(F32) + pos0) * invf_ref[...]
        lane = lax.broadcasted_iota(jnp.int32, (tm, LANES), 1)
        cos_sc[table_rows, :] = jnp.where(lane < rope_dim, jnp.cos(ang), 0.0)
        sin_sc[table_rows, :] = jnp.where(lane < rope_dim, jnp.sin(ang), 0.0)

    cosm = cos_sc[table_rows, :]
    sinm = sin_sc[table_rows, :]

    def rope(grp):
        return grp * cosm + pltpu.roll(grp, LANES - rope_dim, axis=1) * sinm

    mla = _dot(xb, wmla_ref[...])
    q_rank = qg_ref.shape[1]
    ckv = _rmsnorm(mla[:, q_rank:q_rank + LANES], kvg_ref[...])
    kr = rope(mla[:, q_rank + LANES:])
    ckv_out[...] = ckv
    kr_out[...] = kr[:, :rope_dim]
    kvcat_out[:, :LANES] = ckv.astype(BF16)
    kvcat_out[:, LANES:] = kr.astype(BF16)

    cq = _rmsnorm(mla[:, :q_rank], qg_ref[...]).astype(BF16)
    q_nope = _dot(cq, wuqn_ref[...]).astype(BF16)
    q_rg = _dot(cq, wuqr_ref[...])
    nbg, rows = q_out.shape[0], q_out.shape[2]
    for hp in range(n_heads // HEAD_PAIR):
        q_lat = _dot(q_nope[:, hp * LANES:(hp + 1) * LANES], wuk_ref[hp])
        for j in range(HEAD_PAIR):
            h = hp * HEAD_PAIR + j
            q_h = jnp.concatenate([q_lat[:, j * LANES:(j + 1) * LANES] * q_scale,
                                   rope(q_rg[:, h * LANES:(h + 1) * LANES]) * q_scale], axis=-1).astype(BF16)
            q_out[:, h] = q_h.reshape(nbg, rows, 2 * LANES)

    rw = _dot(xb, wrw_ref[...])
    rw_sc[8:tm + 8, :] = rw
    if whole_seqs:
        nb = tm // seq
        rw_sc[7:8, :] = jnp.zeros((1, rw.shape[1]), F32)
        prev = rw_sc[7:tm + 7, :]
        first = jnp.broadcast_to(sb_ref[...], (nb, seq, rw.shape[1])).reshape(tm, rw.shape[1])
        prev = jnp.where(row % seq == 0, first, prev)
        sh_out[...] = rw.reshape(nb, seq, rw.shape[1])[:, seq - 1:seq, :]
    else:
        @pl.when(t == 0)
        def _():
            rw_sc[7:8, :] = sb_ref[0]
        prev = rw_sc[7:tm + 7, :]
        rw_sc[7:8, :] = rw[tm - 1:tm, :]
        sh_out[0] = rw[tm - 1:tm, :]
    rws = rw + (prev - rw) * mu_ref[...]

    r = rws[:, :rw_dim]
    k = rws[:, rw_dim:2 * rw_dim]
    v = rws[:, 2 * rw_dim:3 * rw_dim]
    lora_in = rws[:, 3 * rw_dim:3 * rw_dim + LANES]
    dg = rws[:, 3 * rw_dim + LANES:]
    z = w0_ref[...] + _dot(jnp.tanh(lora_in).astype(BF16), wl_ref[...])
    nz = -z
    softplus = jnp.maximum(nz, 0.0) + jnp.log(1.0 + jnp.exp(-jnp.abs(nz)))
    w = -softplus - 0.5
    a = _sigmoid(a0_ref[...] + _dot(lora_in.astype(BF16), al_ref[...]))
    g = _dot(_sigmoid(dg).astype(BF16), gl_ref[...])
    kk = k * kk_ref[...]
    norm = jnp.sqrt(_head_sums(kk * kk, ones2_ref[...]))
    kk = kk / jnp.maximum(norm, 1e-12)
    r_out[...] = r
    lw_out[...] = -jnp.exp(w)
    k_out[...] = k * (1.0 + (a - 1.0) * ka_ref[...])
    v_out[...] = v
    kkn_out[...] = kk
    as_out[...] = a
    g_out[...] = g


def mixer_in(x, shift_buf, wts, *, n_seq, seq, pos0, tm, n_heads, q_scale, rw_dim, rope_dim):
    n, d = x.shape
    rw_cols = wts["wrw"].shape[1]
    if tm >= seq:
        assert tm % seq == 0 and n % tm == 0
        nb = tm // seq
        grid = (n // tm, 1)
        rows = seq
        nbg = nb
        row_idx = lambda i, t: (i, 0)
        sb_spec = pl.BlockSpec((nb, 1, rw_cols), lambda i, t: (i, 0, 0))
        q_spec = pl.BlockSpec((nb, n_heads, seq, 2 * LANES), lambda i, t: (i, 0, 0, 0))
        q_shape = (n_seq, n_heads, seq, 2 * LANES)
    else:
        assert seq % tm == 0
        tps = seq // tm
        grid = (n_seq, tps)
        rows = tm
        nbg = 1
        row_idx = lambda i, t: (i * tps + t, 0)
        sb_spec = pl.BlockSpec((1, 1, rw_cols), lambda i, t: (i, 0, 0))
        q_spec = pl.BlockSpec((1, n_heads, tm, 2 * LANES), lambda i, t: (i * tps + t, 0, 0, 0))
        q_shape = (n_seq * tps, n_heads, tm, 2 * LANES)
    rowspec = lambda w: pl.BlockSpec((tm, w), row_idx)
    names = ("wmla", "wrw", "qg", "kvg", "wuqn", "wuqr", "wuk", "invf", "mu", "w0", "wl", "a0",
             "al", "gl", "kk", "ka", "ones2")
    consts = [wts[k] for k in names]
    kern = functools.partial(_mixer_in_kernel, seq=seq, tm=tm, pos0=float(pos0), n_heads=n_heads,
                             q_scale=q_scale, rw_dim=rw_dim, rope_dim=rope_dim)
    f = lambda w, dt=F32: jax.ShapeDtypeStruct((n, w), dt)
    out_shape = (jax.ShapeDtypeStruct(q_shape, BF16), f(2 * LANES, BF16), f(LANES), f(rope_dim)) \
        + (f(rw_dim),) * 7 + (jax.ShapeDtypeStruct((n_seq, 1, rw_cols), F32),)
    out_specs = (q_spec, rowspec(2 * LANES), rowspec(LANES), rowspec(rope_dim)) + (rowspec(rw_dim),) * 7 + (sb_spec,)
    return pl.pallas_call(
        kern, grid=grid,
        in_specs=[rowspec(d), sb_spec] + [_const_spec(c.shape) for c in consts],
        out_specs=out_specs, out_shape=out_shape,
        scratch_shapes=[pltpu.VMEM((tm + 8, rw_cols), F32)] + [pltpu.VMEM((max(tm, seq), LANES), F32)] * 2,
        compiler_params=pltpu.CompilerParams(dimension_semantics=("arbitrary", "arbitrary"),
                                             vmem_limit_bytes=VMEM_LIMIT),
        name="mixer_in",
    )(x, shift_buf, *consts)


def _prompt_attn_kernel(q_ref, qn_ref, kv_ref, o_ref, m_sc, acc_sc, s_sc, *, tq, n_heads):
    qb = pl.program_id(1)
    q = q_ref[0]
    m_sc[...] = jnp.full(m_sc.shape, NEG, F32)
    acc_sc[...] = jnp.zeros(acc_sc.shape, F32)
    ones = jnp.ones((tq, LANES), BF16)
    blk = lambda j: pl.multiple_of(j * tq, tq)

    def scores(qv, j, diagonal):
        s = _dot_nt(qv, kv_ref[0, pl.ds(blk(j), tq), :])
        if diagonal:
            qi = lax.broadcasted_iota(jnp.int32, s.shape, 0) % tq
            kj = lax.broadcasted_iota(jnp.int32, s.shape, 1)
            s = jnp.where(qi >= kj, s, NEG)
        return s

    def consume(s, j):
        m_prev = m_sc[...]
        m_new = jnp.maximum(m_prev, jnp.max(s, axis=-1, keepdims=True))
        corr = jnp.exp2(m_prev - m_new)
        p = jnp.exp2(s - jnp.tile(m_new, (1, tq // LANES))).astype(BF16)
        v_ext = jnp.concatenate([kv_ref[0, pl.ds(blk(j), tq), :LANES], ones], axis=-1)
        acc_sc[...] = acc_sc[...] * jnp.tile(corr, (1, 2)) + _dot(p, v_ext)
        m_sc[...] = m_new

    def step(j, next_scores):
        s = s_sc[...]
        s_next = next_scores()
        consume(s, j)
        s_sc[...] = s_next

    @pl.when(qb == 0)
    def _():
        s_sc[...] = scores(q, 0, True)

    def body(j, carry):
        step(j, lambda: scores(q, j + 1, False))
        return carry

    lax.fori_loop(0, qb - 1, body, 0)

    @pl.when(qb > 0)
    def _():
        step(qb - 1, lambda: scores(q, qb, True))

    step(qb, lambda: scores(qn_ref[0], 0, False))
    acc = acc_sc[...]
    o = (acc[:, :LANES] / acc[:, LANES:]).astype(BF16)
    for h in range(n_heads):
        o_ref[0, :, h * LANES:(h + 1) * LANES] = o[h * tq:(h + 1) * tq]


def prompt_attn(q, kvcat, *, n_seq, seq, tq, n_heads):
    nq = seq // tq
    m = n_heads * tq
    assert seq % tq == 0 and tq % LANES == 0
    kern = functools.partial(_prompt_attn_kernel, tq=tq, n_heads=n_heads)
    return pl.pallas_call(
        kern, grid=(n_seq, nq),
        in_specs=[pl.BlockSpec((1, m, 2 * LANES), lambda b, i: (b * nq + i, 0, 0)),
                  pl.BlockSpec((1, m, 2 * LANES), lambda b, i: (b * nq + jnp.minimum(i + 1, nq - 1), 0, 0)),
                  pl.BlockSpec((1, seq, 2 * LANES), lambda b, i: (b, 0, 0))],
        out_specs=pl.BlockSpec((1, tq, n_heads * LANES), lambda b, i: (b, i, 0)),
        out_shape=jax.ShapeDtypeStruct((n_seq, seq, n_heads * LANES), BF16),
        scratch_shapes=[pltpu.VMEM((m, LANES), F32), pltpu.VMEM((m, 2 * LANES), F32), pltpu.VMEM((m, tq), F32)],
        compiler_params=pltpu.CompilerParams(dimension_semantics=("arbitrary", "arbitrary"),
                                             vmem_limit_bytes=VMEM_LIMIT),
        name="prompt_attn",
    )(q, q, kvcat)


def _softmax_update(s, vals, m_sc, l_sc, acc_sc):
    m_prev = m_sc[...]
    m_new = jnp.maximum(m_prev, jnp.max(s, axis=-1, keepdims=True))
    corr = jnp.exp2(m_prev - m_new)
    p = jnp.exp2(s - m_new)
    l_sc[...] = l_sc[...] * corr + jnp.sum(p, axis=-1, keepdims=True)
    acc_sc[...] = acc_sc[...] * corr + _dot(p.astype(BF16), vals)
    m_sc[...] = m_new


SEQS_PER_STEP = 2


def _sample_attn_kernel(pt_ref, q_ref, kvnew_ref, ckv_hbm, kr_hbm, o_ref, ckv_buf, kr_buf, sem, m_sc, l_sc, acc_sc,
                        s_sc, *, n_pages, tile_pages, rope_dim, dec_seq, n_heads, n_seq):
    i = pl.program_id(0)
    tp = tile_pages
    n_tiles = n_pages // tp

    def page_copies(page, slot, p):
        return (pltpu.make_async_copy(ckv_hbm.at[page], ckv_buf.at[slot, p], sem.at[slot, 0]),
                pltpu.make_async_copy(kr_hbm.at[page], kr_buf.at[slot, p], sem.at[slot, 1]))

    def start_gather(seq, slot):
        def body(p, carry):
            for cp in page_copies(pt_ref[seq * n_pages + p], slot, p):
                cp.start()
            return carry
        lax.fori_loop(0, n_pages, body, 0, unroll=8)

    def wait_gather(slot):
        for p in range(n_pages):
            for cp in page_copies(0, slot, p):
                cp.wait()

    def attend(j, slot):
        q = q_ref[j]
        q_lat, q_rope = q[:, :LANES], q[:, LANES:LANES + rope_dim]
        m_sc[...] = jnp.full(m_sc.shape, NEG, F32)
        l_sc[...] = jnp.zeros(l_sc.shape, F32)
        acc_sc[...] = jnp.zeros(acc_sc.shape, F32)

        def vals(t):
            c = ckv_buf[slot, pl.ds(t * tp, tp)]
            return c.reshape(tp * LANES, LANES).astype(BF16)

        def scores(t):
            kr = kr_buf[slot, pl.ds(t * tp, tp)].astype(BF16)
            kr_cat = jnp.concatenate([kr[u] for u in range(tp)], axis=-1)
            return _dot_nt(q_lat, vals(t)) + _dot(q_rope, kr_cat)

        s_sc[...] = scores(0)

        def body(t, carry):
            s = s_sc[...]
            s_next = scores(t + 1)
            _softmax_update(s, vals(t), m_sc, l_sc, acc_sc)
            s_sc[...] = s_next
            return carry

        lax.fori_loop(0, n_tiles - 1, body, 0)
        _softmax_update(s_sc[...], vals(n_tiles - 1), m_sc, l_sc, acc_sc)
        kn = kvnew_ref[j]
        s = _dot_nt(q, kn)
        qi = lax.broadcasted_iota(jnp.int32, s.shape, 0) % dec_seq
        kj = lax.broadcasted_iota(jnp.int32, s.shape, 1)
        _softmax_update(jnp.where(qi >= kj, s, NEG), kn[:, :LANES], m_sc, l_sc, acc_sc)
        o = (acc_sc[...] / l_sc[...]).astype(BF16)
        for h in range(n_heads):
            o_ref[j, :, h * LANES:(h + 1) * LANES] = o[h * dec_seq:(h + 1) * dec_seq]

    first = SEQS_PER_STEP * i

    @pl.when(i == 0)
    def _():
        start_gather(0, 0)

    start_gather(first + 1, 1)
    wait_gather(0)
    attend(0, 0)

    @pl.when(first + 2 < n_seq)
    def _():
        start_gather(first + 2, 0)

    wait_gather(1)
    attend(1, 1)


def sample_attn(q, kvnew, cache_ckv, cache_kr_t, page_table, *, dec_seq, n_heads, tile_pages):
    b, n_pages = page_table.shape
    page = cache_ckv.shape[1]
    rope_dim = cache_kr_t.shape[1]
    assert n_pages % tile_pages == 0 and page == LANES and b % SEQS_PER_STEP == 0
    m = n_heads * dec_seq
    kern = functools.partial(_sample_attn_kernel, n_pages=n_pages, tile_pages=tile_pages, rope_dim=rope_dim,
                             dec_seq=dec_seq, n_heads=n_heads, n_seq=b)
    per_step = lambda rows, width: pl.BlockSpec((SEQS_PER_STEP, rows, width), lambda i, pt: (i, 0, 0))
    grid_spec = pltpu.PrefetchScalarGridSpec(
        num_scalar_prefetch=1, grid=(b // SEQS_PER_STEP,),
        in_specs=[per_step(m, 2 * LANES), per_step(dec_seq, 2 * LANES),
                  pl.BlockSpec(memory_space=pl.ANY), pl.BlockSpec(memory_space=pl.ANY)],
        out_specs=per_step(dec_seq, n_heads * LANES),
        scratch_shapes=[pltpu.VMEM((SEQS_PER_STEP, n_pages, page, LANES), F32),
                        pltpu.VMEM((SEQS_PER_STEP, n_pages, rope_dim, page), F32),
                        pltpu.SemaphoreType.DMA((SEQS_PER_STEP, 2)),
                        pltpu.VMEM((m, 1), F32), pltpu.VMEM((m, 1), F32), pltpu.VMEM((m, LANES), F32),
                        pltpu.VMEM((m, tile_pages * page), F32)])
    return pl.pallas_call(
        kern, grid_spec=grid_spec,
        out_shape=jax.ShapeDtypeStruct((b, dec_seq, n_heads * LANES), BF16),
        compiler_params=pltpu.CompilerParams(dimension_semantics=("arbitrary",), vmem_limit_bytes=VMEM_LIMIT),
        name="sample_attn",
    )(page_table.reshape(-1), q, kvnew, cache_ckv, cache_kr_t)


def _cumsum_rows(tri, x):
    if x.shape[0] < 16:
        return jnp.dot(tri.astype(F32), x, preferred_element_type=F32, precision=lax.Precision.HIGHEST)
    hi = x.astype(BF16)
    lo = (x - hi.astype(F32)).astype(BF16)
    return _dot(tri, hi) + _dot(tri, lo)


def _rwkv_chunk(rs, lws, ks, vs, kks, asigs, s0s, tri, strict, incl, head0):
    n = range(len(rs))
    c = rs[0].shape[0]
    c2 = 2 * c

    def stack(x):
        return jnp.concatenate([jnp.where(head0, x, 0.0), jnp.where(head0, 0.0, x)], axis=0)

    cs = [_cumsum_rows(tri, lws[i]) for i in n]
    cs_end = [cs[i][c - 1:c, :] for i in n]
    e_neg = [jnp.exp(-cs[i]) for i in n]
    e_end = [jnp.exp(cs_end[i] - cs[i]) for i in n]
    bvec = [kks[i] * asigs[i] for i in n]
    a_t = [stack(-kks[i] * jnp.exp(cs[i] - lws[i])) for i in n]
    r_t = [stack(rs[i] * jnp.exp(cs[i])).astype(BF16) for i in n]
    k_t = [stack(ks[i] * e_neg[i]) for i in n]
    b_t = [stack(bvec[i] * e_neg[i]) for i in n]
    k_e = [stack(ks[i] * e_end[i]).astype(BF16) for i in n]
    b_e = [stack(bvec[i] * e_end[i]).astype(BF16) for i in n]
    v_s = [stack(vs[i]).astype(BF16) for i in n]

    sc = [_dot_nt(jnp.concatenate([a_t[i].astype(BF16), r_t[i]], axis=0),
                  jnp.concatenate([k_t[i], b_t[i]], axis=0).astype(BF16)) for i in n]
    l_ak = [jnp.where(strict, sc[i][:c2, :c2], 0.0).astype(BF16) for i in n]
    lp = [jnp.where(strict, sc[i][:c2, c2:], 0.0).astype(BF16) for i in n]
    a_rk = [jnp.where(incl, sc[i][c2:, :c2], 0.0).astype(BF16) for i in n]
    a_rb = [jnp.where(incl, sc[i][c2:, c2:], 0.0).astype(BF16) for i in n]

    x = [jnp.concatenate([a_t[i], _dot(l_ak[i], v_s[i])], axis=-1) for i in n]
    n_fac = int(math.log2(c))
    for f in range(n_fac):
        x = [x[i] + _dot(lp[i], x[i].astype(BF16)) for i in n]
        if f + 1 < n_fac:
            lp = [_dot(lp[i], lp[i]).astype(BF16) for i in n]

    s0b = [s0s[i].astype(BF16) for i in n]
    u = [(_dot_nt(x[i][:, :LANES].astype(BF16), s0b[i]) + x[i][:, LANES:]).astype(BF16) for i in n]
    y_st = [_dot_nt(r_t[i], s0b[i]) + _dot(a_rk[i], v_s[i]) + _dot(a_rb[i], u[i]) for i in n]
    ys = [y_st[i][:c] + y_st[i][c:] for i in n]
    s_new = [s0s[i] * jnp.exp(cs_end[i]) + _dot_tn(v_s[i], k_e[i]) + _dot_tn(u[i], b_e[i]) for i in n]
    return ys, s_new


def _rwkv_kernel(r_ref, lw_ref, k_ref, v_ref, kk_ref, as_ref, g_ref, s0_ref, rk_ref, lg_ref, lb_ref, ones2_ref,
                 y_ref, sout_ref, s_sc, *, chunk, n_pairs, n_seq_blk):
    ci = pl.program_id(1)

    @pl.when(ci == 0)
    def _():
        s_sc[...] = s0_ref[...]

    c = chunk
    ti = lax.broadcasted_iota(jnp.int32, (c, c), 0)
    tj = lax.broadcasted_iota(jnp.int32, (c, c), 1)
    tri = (ti >= tj).astype(BF16)
    si = lax.broadcasted_iota(jnp.int32, (2 * c, 2 * c), 0)
    sj = lax.broadcasted_iota(jnp.int32, (2 * c, 2 * c), 1)
    strict = si > sj
    incl = si >= sj
    head0 = lax.broadcasted_iota(jnp.int32, (c, LANES), 1) < (LANES // HEAD_PAIR)
    ones2 = ones2_ref[...]
    inv_n = 1.0 / (LANES // HEAD_PAIR)

    streams = [(b, p) for b in range(n_seq_blk) for p in range(n_pairs)]
    n = range(len(streams))
    lanes = [slice(p * LANES, (p + 1) * LANES) for _, p in streams]
    get = lambda ref: [ref[b, :, lanes[i]] for i, (b, _) in enumerate(streams)]
    par = lambda ref: [ref[:, lanes[i]] for i in n]
    rs, ks, vs = get(r_ref), get(k_ref), get(v_ref)
    ys, s_new = _rwkv_chunk(rs, get(lw_ref), ks, vs, get(kk_ref), get(as_ref),
                            [s_sc[b, p] for b, p in streams], tri, strict, incl, head0)
    for i, (b, p) in enumerate(streams):
        s_sc[b, p] = s_new[i]
    lg, lb, rk = par(lg_ref), par(lb_ref), par(rk_ref)
    mu = [_head_sums(ys[i], ones2) * inv_n for i in n]
    d = [ys[i] - mu[i] for i in n]
    var = [_head_sums(d[i] * d[i], ones2) * inv_n for i in n]
    bonus = [_head_sums(rs[i] * ks[i] * rk[i], ones2) * vs[i] for i in n]
    gs = get(g_ref)
    for i, (b, p) in enumerate(streams):
        o = d[i] * lax.rsqrt(var[i] + GN_EPS) * lg[i] + lb[i] + bonus[i]
        y_ref[b, :, lanes[i]] = (o * gs[i]).astype(BF16)

    @pl.when(ci == pl.num_programs(1) - 1)
    def _():
        sout_ref[...] = s_sc[...]


def rwkv_chunked(r, lw, k, v, kk, asig, g, s0, rk, lg, lb, ones2, *, chunk, n_seq_blk):
    n_seq, seq, rw_dim = r.shape
    n_pairs = rw_dim // LANES
    nc = seq // chunk
    assert n_seq % n_seq_blk == 0 and seq % chunk == 0
    kern = functools.partial(_rwkv_kernel, chunk=chunk, n_pairs=n_pairs, n_seq_blk=n_seq_blk)
    row = pl.BlockSpec((n_seq_blk, chunk, rw_dim), lambda b, c: (b, c, 0))
    st = pl.BlockSpec((n_seq_blk, n_pairs, LANES, LANES), lambda b, c: (b, 0, 0, 0))
    return pl.pallas_call(
        kern, grid=(n_seq // n_seq_blk, nc),
        in_specs=[row] * 7 + [st] + [_const_spec(x.shape) for x in (rk, lg, lb, ones2)],
        out_specs=(row, st),
        out_shape=(jax.ShapeDtypeStruct((n_seq, seq, rw_dim), BF16), jax.ShapeDtypeStruct(s0.shape, F32)),
        scratch_shapes=[pltpu.VMEM((n_seq_blk, n_pairs, LANES, LANES), F32)],
        compiler_params=pltpu.CompilerParams(dimension_semantics=("arbitrary", "arbitrary"),
                                             vmem_limit_bytes=VMEM_LIMIT),
        name="rwkv_chunked",
    )(r, lw, k, v, kk, asig, g, s0, rk, lg, lb, ones2)


def _block_diag(blocks):
    n, r, c = blocks.shape
    eye = jnp.eye(n, dtype=blocks.dtype)
    return (eye[:, None, :, None] * blocks[:, :, None, :]).reshape(n * r, n * c)


def _rope_group(w, rope_dim):
    half = rope_dim // 2
    x1, x2 = w[..., :half], w[..., half:]
    pad = jnp.zeros(w.shape[:-1] + (LANES - 2 * rope_dim,), w.dtype)
    return jnp.concatenate([x1, x2, -x2, x1, pad], axis=-1)


def _prep_weights(p, dims):
    q_rank, kv_rank, rope_dim, n_heads, nope, rw_dim = (dims[k] for k in
                                                        ("q_rank", "kv_rank", "rope_dim", "n_heads", "nope", "rw_dim"))
    w_in = p["w_in"]
    o1, o2 = q_rank + kv_rank, q_rank + kv_rank + rope_dim
    w_uq = p["w_uq"]
    row = lambda a: a.reshape(1, -1).astype(F32)
    d_wl = p["w_lora_up"].shape[0]
    d_al = p["a_lora_up"].shape[0]
    assert d_wl + d_al == LANES
    half = jnp.arange(0, rope_dim, 2, dtype=F32)
    inv = ROPE_BASE ** (-half / rope_dim)
    invf = jnp.concatenate([inv, inv, jnp.zeros((LANES - rope_dim,), F32)]).reshape(1, LANES)
    rw_head = rw_dim // dims["rw_heads"]
    assert HEAD_PAIR * rw_head == LANES and HEAD_PAIR * nope == LANES and kv_rank == LANES
    pair_ones = _block_diag(jnp.ones((HEAD_PAIR, rw_head, rw_head), F32))
    w_out = p["w_out"]
    d_mla = n_heads * dims["mla_v"]
    wuk = jnp.transpose(p["w_uk"], (1, 2, 0)).reshape(n_heads // HEAD_PAIR, HEAD_PAIR, nope, kv_rank)
    return {
        "wmla": jnp.concatenate([w_in[:, :o1], _rope_group(w_in[:, o1:o2], rope_dim)], axis=1).astype(BF16),
        "wrw": w_in[:, o2:].astype(BF16),
        "qg": row(p["q_norm_g"]), "kvg": row(p["kv_norm_g"]),
        "wuqn": w_uq[:, :, :nope].reshape(q_rank, n_heads * nope).astype(BF16),
        "wuqr": _rope_group(w_uq[:, :, nope:], rope_dim).reshape(q_rank, n_heads * LANES).astype(BF16),
        "wuk": jnp.stack([_block_diag(w) for w in wuk]).astype(BF16),
        "invf": invf,
        "mu": row(p["shift_mu"]), "w0": row(p["w0"]), "a0": row(p["a0"]),
        "wl": jnp.concatenate([p["w_lora_up"], jnp.zeros((d_al, rw_dim), F32)], axis=0).astype(BF16),
        "al": jnp.concatenate([jnp.zeros((d_wl, rw_dim), F32), p["a_lora_up"]], axis=0).astype(BF16),
        "gl": p["g_lora_up"].astype(BF16),
        "kk": row(p["k_k"]), "ka": row(p["k_a"]),
        "ones2": jnp.concatenate([pair_ones, pair_ones], axis=0).astype(BF16),
        "rk": row(p["r_k"]), "lg": row(p["lnx_g"]), "lb": row(p["lnx_b"]),
        "wuv": _block_diag(jnp.transpose(p["w_uv"], (1, 0, 2))).astype(BF16),
        "woa": w_out[:d_mla].astype(BF16), "wob": w_out[d_mla:].astype(BF16),
    }


def _pair_state(s):
    b, h, n, _ = s.shape
    s = s.reshape(b, h // HEAD_PAIR, HEAD_PAIR, n, n)
    eye = jnp.eye(HEAD_PAIR, dtype=s.dtype)
    return (s[:, :, :, :, None, :] * eye[None, None, :, None, :, None]).reshape(b, h // HEAD_PAIR, HEAD_PAIR * n,
                                                                                 HEAD_PAIR * n)


def _unpair_state(s, n):
    b, hp = s.shape[:2]
    s = s.reshape(b, hp, HEAD_PAIR, n, HEAD_PAIR, n)
    return jnp.stack([s[:, :, i, :, i, :] for i in range(HEAD_PAIR)], axis=2).reshape(b, hp * HEAD_PAIR, n, n)


def _group(x, shift_buf, wkv0, wts, ffa, ffb, lns, dims, *, pos0, attend, tm_in, chunk, rw_seq_blk, alpha):
    n_seq, seq, d = x.shape
    n = n_seq * seq
    n_heads, rw_dim, rope_dim = dims["n_heads"], dims["rw_dim"], dims["rope_dim"]
    x1 = ffn_ln(x.reshape(n, d), *ffa, lns["ln1_g"], lns["ln1_b"], alpha=alpha)
    (q, kvcat, ckv, kr, r, lw, k, v, kk, asig, g, shift_out) = mixer_in(
        x1, shift_buf.reshape(n_seq, 1, -1), wts, n_seq=n_seq, seq=seq, pos0=pos0, tm=tm_in, n_heads=n_heads,
        q_scale=dims["q_scale"], rw_dim=rw_dim, rope_dim=rope_dim)
    ocat = attend(q.reshape(q.shape[0], n_heads * q.shape[2], q.shape[3]), kvcat.reshape(n_seq, seq, -1))
    seqs = lambda a: a.reshape(n_seq, seq, -1)
    yrw, s_out = rwkv_chunked(seqs(r), seqs(lw), seqs(k), seqs(v), seqs(kk), seqs(asig), seqs(g), _pair_state(wkv0),
                              wts["rk"], wts["lg"], wts["lb"], wts["ones2"], chunk=chunk, n_seq_blk=rw_seq_blk)
    y = out_ffn_ln(x1, ocat.reshape(n, -1), yrw.reshape(n, -1), wts["wuv"], wts["woa"], wts["wob"], lns["ln2_g"], lns["ln2_b"],
                   *ffb, lns["ln3_g"], lns["ln3_b"], alpha=alpha)
    return (y.reshape(n_seq, seq, d), ckv.reshape(1, n_seq, seq, -1), kr.reshape(1, n_seq, seq, -1),
            _unpair_state(s_out, rw_dim // dims["rw_heads"])[None], shift_out.reshape(1, n_seq, -1))


def kernel(x_prompt, x_sample, cache_ckv, cache_krope, state_wkv, state_shift, page_table, ln1_g, ln1_b, ffa_w_gate, ffa_w_up, ffa_w_down, w_in, q_norm_g, w_uq, kv_norm_g, w_uk, w_uv, shift_mu, w0, w_lora_up, a0, a_lora_up, g_lora_up, k_k, k_a, r_k, lnx_g, lnx_b, w_out, ln2_g, ln2_b, ffb_w_gate, ffb_w_up, ffb_w_down, ln3_g, ln3_b):
    depth = w_in.shape[0]
    assert depth == 1
    alpha = (2 * depth) ** 0.25
    n_heads, nope_rope = w_uq.shape[2], w_uq.shape[3]
    rope_dim = cache_krope.shape[-1]
    rw_heads, rw_head = state_wkv.shape[2], state_wkv.shape[3]
    dims = dict(q_rank=w_uq.shape[1], kv_rank=w_uk.shape[1], rope_dim=rope_dim, n_heads=n_heads,
                nope=nope_rope - rope_dim, rw_dim=rw_heads * rw_head, rw_heads=rw_heads, mla_v=w_uv.shape[3],
                q_scale=float(nope_rope) ** -0.5 * math.log2(math.e))
    p = dict(w_in=w_in[0], q_norm_g=q_norm_g[0], w_uq=w_uq[0], kv_norm_g=kv_norm_g[0], w_uk=w_uk[0], w_uv=w_uv[0],
             shift_mu=shift_mu[0], w0=w0[0], w_lora_up=w_lora_up[0], a0=a0[0], a_lora_up=a_lora_up[0],
             g_lora_up=g_lora_up[0], k_k=k_k[0], k_a=k_a[0], r_k=r_k[0], lnx_g=lnx_g[0], lnx_b=lnx_b[0],
             w_out=w_out[0])
    wts = _prep_weights(p, dims)
    row = lambda a: a[0].reshape(1, -1)
    lns = dict(ln1_g=row(ln1_g), ln1_b=row(ln1_b), ln2_g=row(ln2_g), ln2_b=row(ln2_b), ln3_g=row(ln3_g),
               ln3_b=row(ln3_b))
    ffa = (ffa_w_gate[0].astype(BF16), ffa_w_up[0].astype(BF16), ffa_w_down[0].astype(BF16))
    ffb = (ffb_w_gate[0].astype(BF16), ffb_w_up[0].astype(BF16), ffb_w_down[0].astype(BF16))

    bp, sp, _ = x_prompt.shape
    bs, ss, _ = x_sample.shape
    n_pages, page = page_table.shape[1], cache_ckv.shape[2]
    past_len = n_pages * page
    tq = min(512, sp)

    attend_p = functools.partial(prompt_attn, n_seq=bp, seq=sp, tq=tq, n_heads=n_heads)
    out_p = _group(x_prompt, jnp.zeros((bp, state_shift.shape[-1]), F32),
                   jnp.zeros((bp,) + state_wkv.shape[2:], F32), wts, ffa, ffb, lns, dims,
                   pos0=0.0, attend=attend_p, tm_in=tq, chunk=min(64, sp), rw_seq_blk=math.gcd(bp, 4), alpha=alpha)

    kr_t = jnp.swapaxes(cache_krope[0], 1, 2)
    attend_s = lambda q, kvnew: sample_attn(q, kvnew, cache_ckv[0], kr_t, page_table, dec_seq=ss,
                                            n_heads=n_heads, tile_pages=math.gcd(n_pages, 32))
    out_s = _group(x_sample, state_shift[0], state_wkv[0], wts, ffa, ffb, lns, dims,
                   pos0=float(past_len), attend=attend_s, tm_in=min(128, bs * ss), chunk=ss,
                   rw_seq_blk=math.gcd(bs, 8), alpha=alpha)
    return (out_p[0], out_s[0]) + out_p[1:] + out_s[1:]
```

```python
import functools
import math

import jax
import jax.numpy as jnp
from jax import lax
from jax.experimental import pallas as pl
from jax.experimental.pallas import tpu as pltpu

F32 = jnp.float32
BF16 = jnp.bfloat16

LANES = 128
HEAD_PAIR = 2
LN_EPS = 1e-5
RMS_EPS = 1e-6
GN_EPS = 64e-5
ROPE_BASE = 10000.0
NEG = -1e30
VMEM_LIMIT = 56 * 1024 * 1024


def _dot(a, b):
    return jnp.dot(a, b, preferred_element_type=F32)


def _dot_nt(a, b):
    return lax.dot_general(a, b, (((1,), (1,)), ((), ())), preferred_element_type=F32)


def _dot_tn(a, b):
    return lax.dot_general(a, b, (((0,), (0,)), ((), ())), preferred_element_type=F32)


def _layernorm(y, g, b):
    mu = jnp.mean(y, axis=-1, keepdims=True)
    d = y - mu
    var = jnp.mean(d * d, axis=-1, keepdims=True)
    return d * lax.rsqrt(var + LN_EPS) * g + b


def _rmsnorm(y, g):
    return y * lax.rsqrt(jnp.mean(y * y, axis=-1, keepdims=True) + RMS_EPS) * g


def _sigmoid(x):
    return 1.0 / (1.0 + jnp.exp(-x))


def _head_sums(x, ones2):
    hi = x.astype(BF16)
    lo = (x - hi.astype(F32)).astype(BF16)
    groups = [slice(g * LANES, (g + 1) * LANES) for g in range(x.shape[1] // LANES)]
    return jnp.concatenate([_dot(jnp.concatenate([hi[:, g], lo[:, g]], axis=-1), ones2) for g in groups], axis=-1)


def _const_spec(shape):
    nd = len(shape)
    return pl.BlockSpec(shape, lambda *_: (0,) * nd, pipeline_mode=pl.Buffered(1))


def _swiglu_ln(x, wg_ref, wu_ref, wd_ref, g_ref, b_ref, alpha, ff_chunk):
    xb = x.astype(BF16)
    d_ff = wg_ref.shape[1]
    acc = jnp.zeros(x.shape, F32)
    for c in range(d_ff // ff_chunk):
        sl = slice(c * ff_chunk, (c + 1) * ff_chunk)
        gate = _dot(xb, wg_ref[:, sl])
        up = _dot(xb, wu_ref[:, sl])
        h = (gate * _sigmoid(gate) * up).astype(BF16)
        acc = acc + _dot(h, wd_ref[sl, :])
    return _layernorm(alpha * x + 0.5 * acc, g_ref[...], b_ref[...])


def _ffn_ln_kernel(x_ref, wg_ref, wu_ref, wd_ref, g_ref, b_ref, o_ref, *, alpha, ff_chunk):
    o_ref[...] = _swiglu_ln(x_ref[...], wg_ref, wu_ref, wd_ref, g_ref, b_ref, alpha, ff_chunk)


def _out_ffn_ln_kernel(x_ref, ocat_ref, yrw_ref, wuv_ref, woa_ref, wob_ref, g2_ref, b2_ref,
                       wg_ref, wu_ref, wd_ref, g3_ref, b3_ref, o_ref, *, alpha, ff_chunk):
    y_mla = _dot(ocat_ref[...], wuv_ref[...]).astype(BF16)
    y = _dot(y_mla, woa_ref[...]) + _dot(yrw_ref[...], wob_ref[...])
    x2 = _layernorm(alpha * x_ref[...] + y, g2_ref[...], b2_ref[...])
    o_ref[...] = _swiglu_ln(x2, wg_ref, wu_ref, wd_ref, g3_ref, b3_ref, alpha, ff_chunk)


def _row_tile(n, want):
    t = min(n, want)
    assert n % t == 0
    return t


def _ffn_chunk(d_ff):
    for c in (512, 256, 128):
        if d_ff % c == 0:
            return c
    return d_ff


def ffn_ln(x, wg, wu, wd, g, b, *, alpha, tm=512):
    n, d = x.shape
    tm = _row_tile(n, tm)
    kern = functools.partial(_ffn_ln_kernel, alpha=alpha, ff_chunk=_ffn_chunk(wg.shape[1]))
    row = pl.BlockSpec((tm, d), lambda i: (i, 0))
    return pl.pallas_call(
        kern, grid=(n // tm,),
        in_specs=[row, _const_spec(wg.shape), _const_spec(wu.shape), _const_spec(wd.shape),
                  _const_spec(g.shape), _const_spec(b.shape)],
        out_specs=row, out_shape=jax.ShapeDtypeStruct((n, d), F32),
        compiler_params=pltpu.CompilerParams(dimension_semantics=("arbitrary",), vmem_limit_bytes=VMEM_LIMIT),
        name="ffn_ln",
    )(x, wg, wu, wd, g, b)


def out_ffn_ln(x, ocat, yrw, wuv, woa, wob, g2, b2, wg, wu, wd, g3, b3, *, alpha, tm=512):
    n, d = x.shape
    tm = _row_tile(n, tm)
    kern = functools.partial(_out_ffn_ln_kernel, alpha=alpha, ff_chunk=_ffn_chunk(wg.shape[1]))
    row = lambda w: pl.BlockSpec((tm, w), lambda i: (i, 0))
    consts = (wuv, woa, wob, g2, b2, wg, wu, wd, g3, b3)
    return pl.pallas_call(
        kern, grid=(n // tm,),
        in_specs=[row(d), row(ocat.shape[1]), row(yrw.shape[1])] + [_const_spec(c.shape) for c in consts],
        out_specs=row(d), out_shape=jax.ShapeDtypeStruct((n, d), F32),
        compiler_params=pltpu.CompilerParams(dimension_semantics=("arbitrary",), vmem_limit_bytes=VMEM_LIMIT),
        name="out_ffn_ln",
    )(x, ocat, yrw, *consts)


def _mixer_in_kernel(x_ref, sb_ref, wmla_ref, wrw_ref, qg_ref, kvg_ref, wuqn_ref, wuqr_ref,
                     wuk_ref, invf_ref, mu_ref, w0_ref, wl_ref, a0_ref, al_ref, gl_ref, kk_ref, ka_ref,
                     ones2_ref,
                     q_out, kvcat_out, ckv_out, kr_out, r_out, lw_out, k_out, v_out, kkn_out, as_out, g_out,
                     sh_out, rw_sc, cos_sc, sin_sc, *, seq, tm, pos0, n_heads, q_scale, rw_dim, rope_dim):
    t = pl.program_id(1)
    xb = x_ref[...].astype(BF16)
    whole_seqs = tm >= seq

    row = lax.broadcasted_iota(jnp.int32, (tm, 1), 0)
    table_rows = slice(None) if whole_seqs else pl.ds(pl.multiple_of(t * tm, tm), tm)

    @pl.when(pl.program_id(0) == 0)
    def _():
        row_in_seq = (row % seq) if whole_seqs else (row + t * tm)
        ang = (row_in_seq.astype(F32) + pos0) * invf_ref[...]
        lane = lax.broadcasted_iota(jnp.int32, (tm, LANES), 1)
        cos_sc[table_rows, :] = jnp.where(lane < rope_dim, jnp.cos(ang), 0.0)
        sin_sc[table_rows, :] = jnp.where(lane < rope_dim, jnp.sin(ang), 0.0)

    cosm = cos_sc[table_rows, :]
    sinm = sin_sc[table_rows, :]

    def rope(grp):
        return grp * cosm + pltpu.roll(grp, LANES - rope_dim, axis=1) * sinm

    mla = _dot(xb, wmla_ref[...])
    q_rank = qg_ref.shape[1]
    ckv = _rmsnorm(mla[:, q_rank:q_rank + LANES], kvg_ref[...])
    kr = rope(mla[:, q_rank + LANES:])
    ckv_out[...] = ckv
    kr_out[...] = kr[:, :rope_dim]
    kvcat_out[:, :LANES] = ckv.astype(BF16)
    kvcat_out[:, LANES:] = kr.astype(BF16)

    cq = _rmsnorm(mla[:, :q_rank], qg_ref[...]).astype(BF16)
    q_nope = _dot(cq, wuqn_ref[...]).astype(BF16)
    q_rg = _dot(cq, wuqr_ref[...])
    nbg, rows = q_out.shape[0], q_out.shape[2]
    for hp in range(n_heads // HEAD_PAIR):
        q_lat = _dot(q_nope[:, hp * LANES:(hp + 1) * LANES], wuk_ref[hp])
        for j in range(HEAD_PAIR):
            h = hp * HEAD_PAIR + j
            q_h = jnp.concatenate([q_lat[:, j * LANES:(j + 1) * LANES] * q_scale,
                                   rope(q_rg[:, h * LANES:(h + 1) * LANES]) * q_scale], axis=-1).astype(BF16)
            q_out[:, h] = q_h.reshape(nbg, rows, 2 * LANES)

    rw = _dot(xb, wrw_ref[...])
    rw_sc[8:tm + 8, :] = rw
    if whole_seqs:
        nb = tm // seq
        rw_sc[7:8, :] = jnp.zeros((1, rw.shape[1]), F32)
        prev = rw_sc[7:tm + 7, :]
        first = jnp.broadcast_to(sb_ref[...], (nb, seq, rw.shape[1])).reshape(tm, rw.shape[1])
        prev = jnp.where(row % seq == 0, first, prev)
        sh_out[...] = rw.reshape(nb, seq, rw.shape[1])[:, seq - 1:seq, :]
    else:
        @pl.when(t == 0)
        def _():
            rw_sc[7:8, :] = sb_ref[0]
        prev = rw_sc[7:tm + 7, :]
        rw_sc[7:8, :] = rw[tm - 1:tm, :]
        sh_out[0] = rw[tm - 1:tm, :]
    rws = rw + (prev - rw) * mu_ref[...]

    r = rws[:, :rw_dim]
    k = rws[:, rw_dim:2 * rw_dim]
    v = rws[:, 2 * rw_dim:3 * rw_dim]
    lora_in = rws[:, 3 * rw_dim:3 * rw_dim + LANES]
    dg = rws[:, 3 * rw_dim + LANES:]
    z = w0_ref[...] + _dot(jnp.tanh(lora_in).astype(BF16), wl_ref[...])
    nz = -z
    softplus = jnp.maximum(nz, 0.0) + jnp.log(1.0 + jnp.exp(-jnp.abs(nz)))
    w = -softplus - 0.5
    a = _sigmoid(a0_ref[...] + _dot(lora_in.astype(BF16), al_ref[...]))
    g = _dot(_sigmoid(dg).astype(BF16), gl_ref[...])
    kk = k * kk_ref[...]
    norm = jnp.sqrt(_head_sums(kk * kk, ones2_ref[...]))
    kk = kk / jnp.maximum(norm, 1e-12)
    r_out[...] = r
    lw_out[...] = -jnp.exp(w)
    k_out[...] = k * (1.0 + (a - 1.0) * ka_ref[...])
    v_out[...] = v
    kkn_out[...] = kk
    as_out[...] = a
    g_out[...] = g


def mixer_in(x, shift_buf, wts, *, n_seq, seq, pos0, tm, n_heads, q_scale, rw_dim, rope_dim):
    n, d = x.shape
    rw_cols = wts["wrw"].shape[1]
    if tm >= seq:
        assert tm % seq == 0 and n % tm == 0
        nb = tm // seq
        grid = (n // tm, 1)
        rows = seq
        nbg = nb
        row_idx = lambda i, t: (i, 0)
        sb_spec = pl.BlockSpec((nb, 1, rw_cols), lambda i, t: (i, 0, 0))
        q_spec = pl.BlockSpec((nb, n_heads, seq, 2 * LANES), lambda i, t: (i, 0, 0, 0))
        q_shape = (n_seq, n_heads, seq, 2 * LANES)
    else:
        assert seq % tm == 0
        tps = seq // tm
        grid = (n_seq, tps)
        rows = tm
        nbg = 1
        row_idx = lambda i, t: (i * tps + t, 0)
        sb_spec = pl.BlockSpec((1, 1, rw_cols), lambda i, t: (i, 0, 0))
        q_spec = pl.BlockSpec((1, n_heads, tm, 2 * LANES), lambda i, t: (i * tps + t, 0, 0, 0))
        q_shape = (n_seq * tps, n_heads, tm, 2 * LANES)
    rowspec = lambda w: pl.BlockSpec((tm, w), row_idx)
    names = ("wmla", "wrw", "qg", "kvg", "wuqn", "wuqr", "wuk", "invf", "mu", "w0", "wl", "a0",
             "al", "gl", "kk", "ka", "ones2")
    consts = [wts[k] for k in names]
    kern = functools.partial(_mixer_in_kernel, seq=seq, tm=tm, pos0=float(pos0), n_heads=n_heads,
                             q_scale=q_scale, rw_dim=rw_dim, rope_dim=rope_dim)
    f = lambda w, dt=F32: jax.ShapeDtypeStruct((n, w), dt)
    out_shape = (jax.ShapeDtypeStruct(q_shape, BF16), f(2 * LANES, BF16), f(LANES), f(rope_dim)) \
        + (f(rw_dim),) * 7 + (jax.ShapeDtypeStruct((n_seq, 1, rw_cols), F32),)
    out_specs = (q_spec, rowspec(2 * LANES), rowspec(LANES), rowspec(rope_dim)) + (rowspec(rw_dim),) * 7 + (sb_spec,)
    return pl.pallas_call(
        kern, grid=grid,
        in_specs=[rowspec(d), sb_spec] + [_const_spec(c.shape) for c in consts],
        out_specs=out_specs, out_shape=out_shape,
        scratch_shapes=[pltpu.VMEM((tm + 8, rw_cols), F32)] + [pltpu.VMEM((max(tm, seq), LANES), F32)] * 2,
        compiler_params=pltpu.CompilerParams(dimension_semantics=("arbitrary", "arbitrary"),
                                             vmem_limit_bytes=VMEM_LIMIT),
        name="mixer_in",
    )(x, shift_buf, *consts)


def _prompt_attn_kernel(q_ref, qn_ref, kv_ref, o_ref, m_sc, acc_sc, s_sc, *, tq, n_heads):
    qb = pl.program_id(1)
    q = q_ref[0]
    m_sc[...] = jnp.full(m_sc.shape, NEG, F32)
    acc_sc[...] = jnp.zeros(acc_sc.shape, F32)
    ones = jnp.ones((tq, LANES), BF16)
    blk = lambda j: pl.multiple_of(j * tq, tq)

    def scores(qv, j, diagonal):
        s = _dot_nt(qv, kv_ref[0, pl.ds(blk(j), tq), :])
        if diagonal:
            qi = lax.broadcasted_iota(jnp.int32, s.shape, 0) % tq
            kj = lax.broadcasted_iota(jnp.int32, s.shape, 1)
            s = jnp.where(qi >= kj, s, NEG)
        return s

    def consume(s, j):
        m_prev = m_sc[...]
        m_new = jnp.maximum(m_prev, jnp.max(s, axis=-1, keepdims=True))
        corr = jnp.exp2(m_prev - m_new)
        p = jnp.exp2(s - jnp.tile(m_new, (1, tq // LANES))).astype(BF16)
        v_ext = jnp.concatenate([kv_ref[0, pl.ds(blk(j), tq), :LANES], ones], axis=-1)
        acc_sc[...] = acc_sc[...] * jnp.tile(corr, (1, 2)) + _dot(p, v_ext)
        m_sc[...] = m_new

    def step(j, next_scores):
        s = s_sc[...]
        s_next = next_scores()
        consume(s, j)
        s_sc[...] = s_next

    @pl.when(qb == 0)
    def _():
        s_sc[...] = scores(q, 0, True)

    def body(j, carry):
        step(j, lambda: scores(q, j + 1, False))
        return carry

    lax.fori_loop(0, qb - 1, body, 0)

    @pl.when(qb > 0)
    def _():
        step(qb - 1, lambda: scores(q, qb, True))

    step(qb, lambda: scores(qn_ref[0], 0, False))
    acc = acc_sc[...]
    o = (acc[:, :LANES] / acc[:, LANES:]).astype(BF16)
    for h in range(n_heads):
        o_ref[0, :, h * LANES:(h + 1) * LANES] = o[h * tq:(h + 1) * tq]


def prompt_attn(q, kvcat, *, n_seq, seq, tq, n_heads):
    nq = seq // tq
    m = n_heads * tq
    assert seq % tq == 0 and tq % LANES == 0
    kern = functools.partial(_prompt_attn_kernel, tq=tq, n_heads=n_heads)
    return pl.pallas_call(
        kern, grid=(n_seq, nq),
        in_specs=[pl.BlockSpec((1, m, 2 * LANES), lambda b, i: (b * nq + i, 0, 0)),
                  pl.BlockSpec((1, m, 2 * LANES), lambda b, i: (b * nq + jnp.minimum(i + 1, nq - 1), 0, 0)),
                  pl.BlockSpec((1, seq, 2 * LANES), lambda b, i: (b, 0, 0))],
        out_specs=pl.BlockSpec((1, tq, n_heads * LANES), lambda b, i: (b, i, 0)),
        out_shape=jax.ShapeDtypeStruct((n_seq, seq, n_heads * LANES), BF16),
        scratch_shapes=[pltpu.VMEM((m, LANES), F32), pltpu.VMEM((m, 2 * LANES), F32), pltpu.VMEM((m, tq), F32)],
        compiler_params=pltpu.CompilerParams(dimension_semantics=("arbitrary", "arbitrary"),
                                             vmem_limit_bytes=VMEM_LIMIT),
        name="prompt_attn",
    )(q, q, kvcat)


def _softmax_update(s, vals, m_sc, l_sc, acc_sc):
    m_prev = m_sc[...]
    m_new = jnp.maximum(m_prev, jnp.max(s, axis=-1, keepdims=True))
    corr = jnp.exp2(m_prev - m_new)
    p = jnp.exp2(s - m_new)
    l_sc[...] = l_sc[...] * corr + jnp.sum(p, axis=-1, keepdims=True)
    acc_sc[...] = acc_sc[...] * corr + _dot(p.astype(BF16), vals)
    m_sc[...] = m_new


SEQS_PER_STEP = 4


def _sample_attn_kernel(pt_ref, q_ref, kvnew_ref, ckv_hbm, kr_hbm, o_ref, ckv_buf, kr_buf, sem, m_sc, l_sc, acc_sc,
                        s_sc, *, n_pages, tile_pages, rope_dim, dec_seq, n_heads, n_seq):
    i = pl.program_id(0)
    tp = tile_pages
    n_tiles = n_pages // tp

    def page_copies(page, slot, p):
        return (pltpu.make_async_copy(ckv_hbm.at[page], ckv_buf.at[slot, p], sem.at[slot, 0]),
                pltpu.make_async_copy(kr_hbm.at[page], kr_buf.at[slot, p], sem.at[slot, 1]))

    def start_gather(seq, slot):
        def body(p, carry):
            for cp in page_copies(pt_ref[seq * n_pages + p], slot, p):
                cp.start()
            return carry
        lax.fori_loop(0, n_pages, body, 0, unroll=8)

    def wait_gather(slot):
        for p in range(n_pages):
            for cp in page_copies(0, slot, p):
                cp.wait()

    def attend(j, slot):
        q = q_ref[j]
        q_lat, q_rope = q[:, :LANES], q[:, LANES:LANES + rope_dim]
        m_sc[...] = jnp.full(m_sc.shape, NEG, F32)
        l_sc[...] = jnp.zeros(l_sc.shape, F32)
        acc_sc[...] = jnp.zeros(acc_sc.shape, F32)

        def vals(t):
            c = ckv_buf[slot, pl.ds(t * tp, tp)]
            return c.reshape(tp * LANES, LANES).astype(BF16)

        def scores(t):
            kr = kr_buf[slot, pl.ds(t * tp, tp)].astype(BF16)
            kr_cat = jnp.concatenate([kr[u] for u in range(tp)], axis=-1)
            return _dot_nt(q_lat, vals(t)) + _dot(q_rope, kr_cat)

        s_sc[...] = scores(0)

        def body(t, carry):
            s = s_sc[...]
            s_next = scores(t + 1)
            _softmax_update(s, vals(t), m_sc, l_sc, acc_sc)
            s_sc[...] = s_next
            return carry

        lax.fori_loop(0, n_tiles - 1, body, 0)
        _softmax_update(s_sc[...], vals(n_tiles - 1), m_sc, l_sc, acc_sc)
        kn = kvnew_ref[j]
        s = _dot_nt(q, kn)
        qi = lax.broadcasted_iota(jnp.int32, s.shape, 0) % dec_seq
        kj = lax.broadcasted_iota(jnp.int32, s.shape, 1)
        _softmax_update(jnp.where(qi >= kj, s, NEG), kn[:, :LANES], m_sc, l_sc, acc_sc)
        o = (acc_sc[...] / l_sc[...]).astype(BF16)
        for h in range(n_heads):
            o_ref[j, :, h * LANES:(h + 1) * LANES] = o[h * dec_seq:(h + 1) * dec_seq]

    first = SEQS_PER_STEP * i

    @pl.when(i == 0)
    def _():
        for j in range(SEQS_PER_STEP):
            start_gather(j, j)

    for j in range(SEQS_PER_STEP):
        wait_gather(j)
        attend(j, j)

        @pl.when(first + SEQS_PER_STEP + j < n_seq)
        def _():
            start_gather(first + SEQS_PER_STEP + j, j)


def sample_attn(q, kvnew, cache_ckv, cache_kr_t, page_table, *, dec_seq, n_heads, tile_pages):
    b, n_pages = page_table.shape
    page = cache_ckv.shape[1]
    rope_dim = cache_kr_t.shape[1]
    assert n_pages % tile_pages == 0 and page == LANES and b % SEQS_PER_STEP == 0
    m = n_heads * dec_seq
    kern = functools.partial(_sample_attn_kernel, n_pages=n_pages, tile_pages=tile_pages, rope_dim=rope_dim,
                             dec_seq=dec_seq, n_heads=n_heads, n_seq=b)
    per_step = lambda rows, width: pl.BlockSpec((SEQS_PER_STEP, rows, width), lambda i, pt: (i, 0, 0))
    grid_spec = pltpu.PrefetchScalarGridSpec(
        num_scalar_prefetch=1, grid=(b // SEQS_PER_STEP,),
        in_specs=[per_step(m, 2 * LANES), per_step(dec_seq, 2 * LANES),
                  pl.BlockSpec(memory_space=pl.ANY), pl.BlockSpec(memory_space=pl.ANY)],
        out_specs=per_step(dec_seq, n_heads * LANES),
        scratch_shapes=[pltpu.VMEM((SEQS_PER_STEP, n_pages, page, LANES), F32),
                        pltpu.VMEM((SEQS_PER_STEP, n_pages, rope_dim, page), F32),
                        pltpu.SemaphoreType.DMA((SEQS_PER_STEP, 2)),
                        pltpu.VMEM((m, 1), F32), pltpu.VMEM((m, 1), F32), pltpu.VMEM((m, LANES), F32),
                        pltpu.VMEM((m, tile_pages * page), F32)])
    return pl.pallas_call(
        kern, grid_spec=grid_spec,
        out_shape=jax.ShapeDtypeStruct((b, dec_seq, n_heads * LANES), BF16),
        compiler_params=pltpu.CompilerParams(dimension_semantics=("arbitrary",), vmem_limit_bytes=VMEM_LIMIT),
        name="sample_attn",
    )(page_table.reshape(-1), q, kvnew, cache_ckv, cache_kr_t)


def _cumsum_rows(tri, x):
    if x.shape[0] < 16:
        return jnp.dot(tri.astype(F32), x, preferred_element_type=F32, precision=lax.Precision.HIGHEST)
    hi = x.astype(BF16)
    lo = (x - hi.astype(F32)).astype(BF16)
    both = _dot(tri, jnp.concatenate([hi, lo], axis=-1))
    return both[:, :x.shape[1]] + both[:, x.shape[1]:]


def _rwkv_chunk(rs, lws, ks, vs, kks, asigs, s0s, tri, strict, incl, head0):
    n = range(len(rs))
    c = rs[0].shape[0]
    c2 = 2 * c

    def stack(x):
        return jnp.concatenate([jnp.where(head0, x, 0.0), jnp.where(head0, 0.0, x)], axis=0)

    cs = [_cumsum_rows(tri, lws[i]) for i in n]
    cs_end = [cs[i][c - 1:c, :] for i in n]
    e_neg = [jnp.exp(-cs[i]) for i in n]
    e_end = [jnp.exp(cs_end[i] - cs[i]) for i in n]
    bvec = [kks[i] * asigs[i] for i in n]
    a_t = [stack(-kks[i] * jnp.exp(cs[i] - lws[i])) for i in n]
    r_t = [stack(rs[i] * jnp.exp(cs[i])).astype(BF16) for i in n]
    k_t = [stack(ks[i] * e_neg[i]) for i in n]
    b_t = [stack(bvec[i] * e_neg[i]) for i in n]
    k_e = [stack(ks[i] * e_end[i]).astype(BF16) for i in n]
    b_e = [stack(bvec[i] * e_end[i]).astype(BF16) for i in n]
    v_s = [stack(vs[i]).astype(BF16) for i in n]

    sc = [_dot_nt(jnp.concatenate([a_t[i].astype(BF16), r_t[i]], axis=0),
                  jnp.concatenate([k_t[i], b_t[i]], axis=0).astype(BF16)) for i in n]
    l_ak = [jnp.where(strict, sc[i][:c2, :c2], 0.0).astype(BF16) for i in n]
    lp = [jnp.where(strict, sc[i][:c2, c2:], 0.0).astype(BF16) for i in n]
    a_rk = [jnp.where(incl, sc[i][c2:, :c2], 0.0).astype(BF16) for i in n]
    a_rb = [jnp.where(incl, sc[i][c2:, c2:], 0.0).astype(BF16) for i in n]

    x = [jnp.concatenate([a_t[i], _dot(l_ak[i], v_s[i])], axis=-1) for i in n]
    n_fac = int(math.log2(c))
    for f in range(n_fac):
        x = [x[i] + _dot(lp[i], x[i].astype(BF16)) for i in n]
        if f + 1 < n_fac:
            lp = [_dot(lp[i], lp[i]).astype(BF16) for i in n]

    s0b = [s0s[i].astype(BF16) for i in n]
    ar = [_dot_nt(jnp.concatenate([x[i][:, :LANES].astype(BF16), r_t[i]], axis=0), s0b[i]) for i in n]
    u = [(ar[i][:c2] + x[i][:, LANES:]).astype(BF16) for i in n]
    vu = [jnp.concatenate([v_s[i], u[i]], axis=0) for i in n]
    y_st = [ar[i][c2:] + _dot(jnp.concatenate([a_rk[i], a_rb[i]], axis=-1), vu[i]) for i in n]
    ys = [y_st[i][:c] + y_st[i][c:] for i in n]
    s_new = [s0s[i] * jnp.exp(cs_end[i]) + _dot_tn(vu[i], jnp.concatenate([k_e[i], b_e[i]], axis=0)) for i in n]
    return ys, s_new


def _rwkv_kernel(r_ref, lw_ref, k_ref, v_ref, kk_ref, as_ref, g_ref, s0_ref, rk_ref, lg_ref, lb_ref, ones2_ref,
                 y_ref, sout_ref, s_sc, *, chunk, n_pairs, n_seq_blk):
    ci = pl.program_id(1)

    @pl.when(ci == 0)
    def _():
        s_sc[...] = s0_ref[...]

    c = chunk
    ti = lax.broadcasted_iota(jnp.int32, (c, c), 0)
    tj = lax.broadcasted_iota(jnp.int32, (c, c), 1)
    tri = (ti >= tj).astype(BF16)
    si = lax.broadcasted_iota(jnp.int32, (2 * c, 2 * c), 0)
    sj = lax.broadcasted_iota(jnp.int32, (2 * c, 2 * c), 1)
    strict = si > sj
    incl = si >= sj
    head0 = lax.broadcasted_iota(jnp.int32, (c, LANES), 1) < (LANES // HEAD_PAIR)
    ones2 = ones2_ref[...]
    inv_n = 1.0 / (LANES // HEAD_PAIR)

    streams = [(b, p) for b in range(n_seq_blk) for p in range(n_pairs)]
    n = range(len(streams))
    lanes = [slice(p * LANES, (p + 1) * LANES) for _, p in streams]
    get = lambda ref: [ref[b, :, lanes[i]] for i, (b, _) in enumerate(streams)]
    par = lambda ref: [ref[:, lanes[i]] for i in n]
    rs, ks, vs = get(r_ref), get(k_ref), get(v_ref)
    ys, s_new = _rwkv_chunk(rs, get(lw_ref), ks, vs, get(kk_ref), get(as_ref),
                            [s_sc[b, p] for b, p in streams], tri, strict, incl, head0)
    for i, (b, p) in enumerate(streams):
        s_sc[b, p] = s_new[i]
    lg, lb, rk = par(lg_ref), par(lb_ref), par(rk_ref)
    mu = [_head_sums(ys[i], ones2) * inv_n for i in n]
    d = [ys[i] - mu[i] for i in n]
    var = [_head_sums(d[i] * d[i], ones2) * inv_n for i in n]
    bonus = [_head_sums(rs[i] * ks[i] * rk[i], ones2) * vs[i] for i in n]
    gs = get(g_ref)
    for i, (b, p) in enumerate(streams):
        o = d[i] * lax.rsqrt(var[i] + GN_EPS) * lg[i] + lb[i] + bonus[i]
        y_ref[b, :, lanes[i]] = (o * gs[i]).astype(BF16)

    @pl.when(ci == pl.num_programs(1) - 1)
    def _():
        sout_ref[...] = s_sc[...]


def rwkv_chunked(r, lw, k, v, kk, asig, g, s0, rk, lg, lb, ones2, *, chunk, n_seq_blk):
    n_seq, seq, rw_dim = r.shape
    n_pairs = rw_dim // LANES
    nc = seq // chunk
    assert n_seq % n_seq_blk == 0 and seq % chunk == 0
    kern = functools.partial(_rwkv_kernel, chunk=chunk, n_pairs=n_pairs, n_seq_blk=n_seq_blk)
    row = pl.BlockSpec((n_seq_blk, chunk, rw_dim), lambda b, c: (b, c, 0))
    st = pl.BlockSpec((n_seq_blk, n_pairs, LANES, LANES), lambda b, c: (b, 0, 0, 0))
    return pl.pallas_call(
        kern, grid=(n_seq // n_seq_blk, nc),
        in_specs=[row] * 7 + [st] + [_const_spec(x.shape) for x in (rk, lg, lb, ones2)],
        out_specs=(row, st),
        out_shape=(jax.ShapeDtypeStruct((n_seq, seq, rw_dim), BF16), jax.ShapeDtypeStruct(s0.shape, F32)),
        scratch_shapes=[pltpu.VMEM((n_seq_blk, n_pairs, LANES, LANES), F32)],
        compiler_params=pltpu.CompilerParams(dimension_semantics=("arbitrary", "arbitrary"),
                                             vmem_limit_bytes=VMEM_LIMIT),
        name="rwkv_chunked",
    )(r, lw, k, v, kk, asig, g, s0, rk, lg, lb, ones2)


def _block_diag(blocks):
    n, r, c = blocks.shape
    eye = jnp.eye(n, dtype=blocks.dtype)
    return (eye[:, None, :, None] * blocks[:, :, None, :]).reshape(n * r, n * c)


def _rope_group(w, rope_dim):
    half = rope_dim // 2
    x1, x2 = w[..., :half], w[..., half:]
    pad = jnp.zeros(w.shape[:-1] + (LANES - 2 * rope_dim,), w.dtype)
    return jnp.concatenate([x1, x2, -x2, x1, pad], axis=-1)


def _prep_weights(p, dims):
    q_rank, kv_rank, rope_dim, n_heads, nope, rw_dim = (dims[k] for k in
                                                        ("q_rank", "kv_rank", "rope_dim", "n_heads", "nope", "rw_dim"))
    w_in = p["w_in"]
    o1, o2 = q_rank + kv_rank, q_rank + kv_rank + rope_dim
    w_uq = p["w_uq"]
    row = lambda a: a.reshape(1, -1).astype(F32)
    d_wl = p["w_lora_up"].shape[0]
    d_al = p["a_lora_up"].shape[0]
    assert d_wl + d_al == LANES
    half = jnp.arange(0, rope_dim, 2, dtype=F32)
    inv = ROPE_BASE ** (-half / rope_dim)
    invf = jnp.concatenate([inv, inv, jnp.zeros((LANES - rope_dim,), F32)]).reshape(1, LANES)
    rw_head = rw_dim // dims["rw_heads"]
    assert HEAD_PAIR * rw_head == LANES and HEAD_PAIR * nope == LANES and kv_rank == LANES
    pair_ones = _block_diag(jnp.ones((HEAD_PAIR, rw_head, rw_head), F32))
    w_out = p["w_out"]
    d_mla = n_heads * dims["mla_v"]
    wuk = jnp.transpose(p["w_uk"], (1, 2, 0)).reshape(n_heads // HEAD_PAIR, HEAD_PAIR, nope, kv_rank)
    return {
        "wmla": jnp.concatenate([w_in[:, :o1], _rope_group(w_in[:, o1:o2], rope_dim)], axis=1).astype(BF16),
        "wrw": w_in[:, o2:].astype(BF16),
        "qg": row(p["q_norm_g"]), "kvg": row(p["kv_norm_g"]),
        "wuqn": w_uq[:, :, :nope].reshape(q_rank, n_heads * nope).astype(BF16),
        "wuqr": _rope_group(w_uq[:, :, nope:], rope_dim).reshape(q_rank, n_heads * LANES).astype(BF16),
        "wuk": jnp.stack([_block_diag(w) for w in wuk]).astype(BF16),
        "invf": invf,
        "mu": row(p["shift_mu"]), "w0": row(p["w0"]), "a0": row(p["a0"]),
        "wl": jnp.concatenate([p["w_lora_up"], jnp.zeros((d_al, rw_dim), F32)], axis=0).astype(BF16),
        "al": jnp.concatenate([jnp.zeros((d_wl, rw_dim), F32), p["a_lora_up"]], axis=0).astype(BF16),
        "gl": p["g_lora_up"].astype(BF16),
        "kk": row(p["k_k"]), "ka": row(p["k_a"]),
        "ones2": jnp.concatenate([pair_ones, pair_ones], axis=0).astype(BF16),
        "rk": row(p["r_k"]), "lg": row(p["lnx_g"]), "lb": row(p["lnx_b"]),
        "wuv": _block_diag(jnp.transpose(p["w_uv"], (1, 0, 2))).astype(BF16),
        "woa": w_out[:d_mla].astype(BF16), "wob": w_out[d_mla:].astype(BF16),
    }


def _pair_state(s):
    b, h, n, _ = s.shape
    s = s.reshape(b, h // HEAD_PAIR, HEAD_PAIR, n, n)
    eye = jnp.eye(HEAD_PAIR, dtype=s.dtype)
    return (s[:, :, :, :, None, :] * eye[None, None, :, None, :, None]).reshape(b, h // HEAD_PAIR, HEAD_PAIR * n,
                                                                                 HEAD_PAIR * n)


def _unpair_state(s, n):
    b, hp = s.shape[:2]
    s = s.reshape(b, hp, HEAD_PAIR, n, HEAD_PAIR, n)
    return jnp.stack([s[:, :, i, :, i, :] for i in range(HEAD_PAIR)], axis=2).reshape(b, hp * HEAD_PAIR, n, n)


def _group(x, shift_buf, wkv0, wts, ffa, ffb, lns, dims, *, pos0, attend, tm_in, chunk, rw_seq_blk, alpha):
    n_seq, seq, d = x.shape
    n = n_seq * seq
    n_heads, rw_dim, rope_dim = dims["n_heads"], dims["rw_dim"], dims["rope_dim"]
    x1 = ffn_ln(x.reshape(n, d), *ffa, lns["ln1_g"], lns["ln1_b"], alpha=alpha)
    (q, kvcat, ckv, kr, r, lw, k, v, kk, asig, g, shift_out) = mixer_in(
        x1, shift_buf.reshape(n_seq, 1, -1), wts, n_seq=n_seq, seq=seq, pos0=pos0, tm=tm_in, n_heads=n_heads,
        q_scale=dims["q_scale"], rw_dim=rw_dim, rope_dim=rope_dim)
    ocat = attend(q.reshape(q.shape[0], n_heads * q.shape[2], q.shape[3]), kvcat.reshape(n_seq, seq, -1))
    seqs = lambda a: a.reshape(n_seq, seq, -1)
    yrw, s_out = rwkv_chunked(seqs(r), seqs(lw), seqs(k), seqs(v), seqs(kk), seqs(asig), seqs(g), _pair_state(wkv0),
                              wts["rk"], wts["lg"], wts["lb"], wts["ones2"], chunk=chunk, n_seq_blk=rw_seq_blk)
    y = out_ffn_ln(x1, ocat.reshape(n, -1), yrw.reshape(n, -1), wts["wuv"], wts["woa"], wts["wob"], lns["ln2_g"], lns["ln2_b"],
                   *ffb, lns["ln3_g"], lns["ln3_b"], alpha=alpha)
    return (y.reshape(n_seq, seq, d), ckv.reshape(1, n_seq, seq, -1), kr.reshape(1, n_seq, seq, -1),
            _unpair_state(s_out, rw_dim // dims["rw_heads"])[None], shift_out.reshape(1, n_seq, -1))


def kernel(x_prompt, x_sample, cache_ckv, cache_krope, state_wkv, state_shift, page_table, ln1_g, ln1_b, ffa_w_gate, ffa_w_up, ffa_w_down, w_in, q_norm_g, w_uq, kv_norm_g, w_uk, w_uv, shift_mu, w0, w_lora_up, a0, a_lora_up, g_lora_up, k_k, k_a, r_k, lnx_g, lnx_b, w_out, ln2_g, ln2_b, ffb_w_gate, ffb_w_up, ffb_w_down, ln3_g, ln3_b):
    depth = w_in.shape[0]
    assert depth == 1
    alpha = (2 * depth) ** 0.25
    n_heads, nope_rope = w_uq.shape[2], w_uq.shape[3]
    rope_dim = cache_krope.shape[-1]
    rw_heads, rw_head = state_wkv.shape[2], state_wkv.shape[3]
    dims = dict(q_rank=w_uq.shape[1], kv_rank=w_uk.shape[1], rope_dim=rope_dim, n_heads=n_heads,
                nope=nope_rope - rope_dim, rw_dim=rw_heads * rw_head, rw_heads=rw_heads, mla_v=w_uv.shape[3],
                q_scale=float(nope_rope) ** -0.5 * math.log2(math.e))
    p = dict(w_in=w_in[0], q_norm_g=q_norm_g[0], w_uq=w_uq[0], kv_norm_g=kv_norm_g[0], w_uk=w_uk[0], w_uv=w_uv[0],
             shift_mu=shift_mu[0], w0=w0[0], w_lora_up=w_lora_up[0], a0=a0[0], a_lora_up=a_lora_up[0],
             g_lora_up=g_lora_up[0], k_k=k_k[0], k_a=k_a[0], r_k=r_k[0], lnx_g=lnx_g[0], lnx_b=lnx_b[0],
             w_out=w_out[0])
    wts = _prep_weights(p, dims)
    row = lambda a: a[0].reshape(1, -1)
    lns = dict(ln1_g=row(ln1_g), ln1_b=row(ln1_b), ln2_g=row(ln2_g), ln2_b=row(ln2_b), ln3_g=row(ln3_g),
               ln3_b=row(ln3_b))
    ffa = (ffa_w_gate[0].astype(BF16), ffa_w_up[0].astype(BF16), ffa_w_down[0].astype(BF16))
    ffb = (ffb_w_gate[0].astype(BF16), ffb_w_up[0].astype(BF16), ffb_w_down[0].astype(BF16))

    bp, sp, _ = x_prompt.shape
    bs, ss, _ = x_sample.shape
    n_pages, page = page_table.shape[1], cache_ckv.shape[2]
    past_len = n_pages * page
    tq = min(512, sp)

    attend_p = functools.partial(prompt_attn, n_seq=bp, seq=sp, tq=tq, n_heads=n_heads)
    out_p = _group(x_prompt, jnp.zeros((bp, state_shift.shape[-1]), F32),
                   jnp.zeros((bp,) + state_wkv.shape[2:], F32), wts, ffa, ffb, lns, dims,
                   pos0=0.0, attend=attend_p, tm_in=tq, chunk=min(64, sp), rw_seq_blk=math.gcd(bp, 4), alpha=alpha)

    kr_t = jnp.swapaxes(cache_krope[0], 1, 2)
    attend_s = lambda q, kvnew: sample_attn(q, kvnew, cache_ckv[0], kr_t, page_table, dec_seq=ss,
                                            n_heads=n_heads, tile_pages=math.gcd(n_pages, 32))
    out_s = _group(x_sample, state_shift[0], state_wkv[0], wts, ffa, ffb, lns, dims,
                   pos0=float(past_len), attend=attend_s, tm_in=min(128, bs * ss), chunk=ss,
                   rw_seq_blk=math.gcd(bs, 8), alpha=alpha)
    return (out_p[0], out_s[0]) + out_p[1:] + out_s[1:]
```

```python
import functools
import math

import jax
import jax.numpy as jnp
from jax import lax
from jax.experimental import pallas as pl
from jax.experimental.pallas import tpu as pltpu

F32 = jnp.float32
BF16 = jnp.bfloat16

LANES = 128
HEAD_PAIR = 2
LN_EPS = 1e-5
RMS_EPS = 1e-6
GN_EPS = 64e-5
ROPE_BASE = 10000.0
NEG = -1e30
VMEM_LIMIT = 56 * 1024 * 1024


def _dot(a, b):
    return jnp.dot(a, b, preferred_element_type=F32)


def _dot_nt(a, b):
    return lax.dot_general(a, b, (((1,), (1,)), ((), ())), preferred_element_type=F32)


def _dot_tn(a, b):
    return lax.dot_general(a, b, (((0,), (0,)), ((), ())), preferred_element_type=F32)


def _layernorm(y, g, b):
    mu = jnp.mean(y, axis=-1, keepdims=True)
    d = y - mu
    var = jnp.mean(d * d, axis=-1, keepdims=True)
    return d * lax.rsqrt(var + LN_EPS) * g + b


def _rmsnorm(y, g):
    return y * lax.rsqrt(jnp.mean(y * y, axis=-1, keepdims=True) + RMS_EPS) * g


def _sigmoid(x):
    return 1.0 / (1.0 + jnp.exp(-x))


def _head_sums(x, ones2):
    hi = x.astype(BF16)
    lo = (x - hi.astype(F32)).astype(BF16)
    groups = [slice(g * LANES, (g + 1) * LANES) for g in range(x.shape[1] // LANES)]
    return jnp.concatenate([_dot(jnp.concatenate([hi[:, g], lo[:, g]], axis=-1), ones2) for g in groups], axis=-1)


def _const_spec(shape):
    nd = len(shape)
    return pl.BlockSpec(shape, lambda *_: (0,) * nd, pipeline_mode=pl.Buffered(1))


def _swiglu_ln(x, wg_ref, wu_ref, wd_ref, g_ref, b_ref, alpha, ff_chunk):
    xb = x.astype(BF16)
    d_ff = wg_ref.shape[1]
    acc = jnp.zeros(x.shape, F32)
    for c in range(d_ff // ff_chunk):
        sl = slice(c * ff_chunk, (c + 1) * ff_chunk)
        gate = _dot(xb, wg_ref[:, sl])
        up = _dot(xb, wu_ref[:, sl])
        h = (gate * _sigmoid(gate) * up).astype(BF16)
        acc = acc + _dot(h, wd_ref[sl, :])
    return _layernorm(alpha * x + 0.5 * acc, g_ref[...], b_ref[...])


def _ffn_ln_kernel(x_ref, wg_ref, wu_ref, wd_ref, g_ref, b_ref, o_ref, *, alpha, ff_chunk):
    o_ref[...] = _swiglu_ln(x_ref[...], wg_ref, wu_ref, wd_ref, g_ref, b_ref, alpha, ff_chunk)


def _out_ffn_ln_kernel(x_ref, ocat_ref, yrw_ref, wuv_ref, woa_ref, wob_ref, g2_ref, b2_ref,
                       wg_ref, wu_ref, wd_ref, g3_ref, b3_ref, o_ref, *, alpha, ff_chunk):
    pair_w = wuv_ref.shape[1]
    y_mla = jnp.concatenate([_dot(ocat_ref[:, hp * pair_w:(hp + 1) * pair_w], wuv_ref[hp])
                             for hp in range(wuv_ref.shape[0])], axis=-1).astype(BF16)
    y = _dot(y_mla, woa_ref[...]) + _dot(yrw_ref[...], wob_ref[...])
    x2 = _layernorm(alpha * x_ref[...] + y, g2_ref[...], b2_ref[...])
    o_ref[...] = _swiglu_ln(x2, wg_ref, wu_ref, wd_ref, g3_ref, b3_ref, alpha, ff_chunk)


def _row_tile(n, want):
    t = min(n, want)
    assert n % t == 0
    return t


def _ffn_chunk(d_ff):
    for c in (512, 256, 128):
        if d_ff % c == 0:
            return c
    return d_ff


def ffn_ln(x, wg, wu, wd, g, b, *, alpha, tm=512):
    n, d = x.shape
    tm = _row_tile(n, tm)
    kern = functools.partial(_ffn_ln_kernel, alpha=alpha, ff_chunk=_ffn_chunk(wg.shape[1]))
    row = pl.BlockSpec((tm, d), lambda i: (i, 0))
    return pl.pallas_call(
        kern, grid=(n // tm,),
        in_specs=[row, _const_spec(wg.shape), _const_spec(wu.shape), _const_spec(wd.shape),
                  _const_spec(g.shape), _const_spec(b.shape)],
        out_specs=row, out_shape=jax.ShapeDtypeStruct((n, d), F32),
        compiler_params=pltpu.CompilerParams(dimension_semantics=("arbitrary",), vmem_limit_bytes=VMEM_LIMIT),
        name="ffn_ln",
    )(x, wg, wu, wd, g, b)


def out_ffn_ln(x, ocat, yrw, wuv, woa, wob, g2, b2, wg, wu, wd, g3, b3, *, alpha, tm=512):
    n, d = x.shape
    tm = _row_tile(n, tm)
    kern = functools.partial(_out_ffn_ln_kernel, alpha=alpha, ff_chunk=_ffn_chunk(wg.shape[1]))
    row = lambda w: pl.BlockSpec((tm, w), lambda i: (i, 0))
    consts = (wuv, woa, wob, g2, b2, wg, wu, wd, g3, b3)
    return pl.pallas_call(
        kern, grid=(n // tm,),
        in_specs=[row(d), row(ocat.shape[1]), row(yrw.shape[1])] + [_const_spec(c.shape) for c in consts],
        out_specs=row(d), out_shape=jax.ShapeDtypeStruct((n, d), F32),
        compiler_params=pltpu.CompilerParams(dimension_semantics=("arbitrary",), vmem_limit_bytes=VMEM_LIMIT),
        name="out_ffn_ln",
    )(x, ocat, yrw, *consts)


def _mixer_in_kernel(x_ref, sb_ref, wmla_ref, wrw_ref, qg_ref, kvg_ref, wuqn_ref, wuqr_ref,
                     wuk_ref, invf_ref, mu_ref, w0_ref, wl_ref, a0_ref, al_ref, gl_ref, kk_ref, ka_ref,
                     ones2_ref,
                     q_out, kvcat_out, ckv_out, kr_out, r_out, lw_out, k_out, v_out, kkn_out, as_out, g_out,
                     sh_out, rw_sc, cos_sc, sin_sc, *, seq, tm, pos0, n_heads, q_scale, rw_dim, rope_dim):
    t = pl.program_id(1)
    xb = x_ref[...].astype(BF16)
    whole_seqs = tm >= seq

    row = lax.broadcasted_iota(jnp.int32, (tm, 1), 0)
    table_rows = slice(None) if whole_seqs else pl.ds(pl.multiple_of(t * tm, tm), tm)

    @pl.when(pl.program_id(0) == 0)
    def _():
        row_in_seq = (row % seq) if whole_seqs else (row + t * tm)
        ang = (row_in_seq.astype(F32) + pos0) * invf_ref[...]
        lane = lax.broadcasted_iota(jnp.int32, (tm, LANES), 1)
        cos_sc[table_rows, :] = jnp.where(lane < rope_dim, jnp.cos(ang), 0.0)
        sin_sc[table_rows, :] = jnp.where(lane < rope_dim, jnp.sin(ang), 0.0)

    cosm = cos_sc[table_rows, :]
    sinm = sin_sc[table_rows, :]

    def rope(grp):
        return grp * cosm + pltpu.roll(grp, LANES - rope_dim, axis=1) * sinm

    mla = _dot(xb, wmla_ref[...])
    q_rank = qg_ref.shape[1]
    ckv = _rmsnorm(mla[:, q_rank:q_rank + LANES], kvg_ref[...])
    kr = rope(mla[:, q_rank + LANES:])
    ckv_out[...] = ckv
    kr_out[...] = kr[:, :rope_dim]
    kvcat_out[:, :LANES] = ckv.astype(BF16)
    kvcat_out[:, LANES:] = kr.astype(BF16)

    cq = _rmsnorm(mla[:, :q_rank], qg_ref[...]).astype(BF16)
    q_nope = _dot(cq, wuqn_ref[...]).astype(BF16)
    q_rg = _dot(cq, wuqr_ref[...])
    nbg, rows = q_out.shape[0], q_out.shape[2]
    for hp in range(n_heads // HEAD_PAIR):
        q_lat = _dot(q_nope[:, hp * LANES:(hp + 1) * LANES], wuk_ref[hp])
        for j in range(HEAD_PAIR):
            h = hp * HEAD_PAIR + j
            q_h = jnp.concatenate([q_lat[:, j * LANES:(j + 1) * LANES] * q_scale,
                                   rope(q_rg[:, h * LANES:(h + 1) * LANES]) * q_scale], axis=-1).astype(BF16)
            q_out[:, h] = q_h.reshape(nbg, rows, 2 * LANES)

    rw = _dot(xb, wrw_ref[...])
    rw_sc[8:tm + 8, :] = rw
    if whole_seqs:
        nb = tm // seq
        rw_sc[7:8, :] = jnp.zeros((1, rw.shape[1]), F32)
        prev = rw_sc[7:tm + 7, :]
        first = jnp.broadcast_to(sb_ref[...], (nb, seq, rw.shape[1])).reshape(tm, rw.shape[1])
        prev = jnp.where(row % seq == 0, first, prev)
        sh_out[...] = rw.reshape(nb, seq, rw.shape[1])[:, seq - 1:seq, :]
    else:
        @pl.when(t == 0)
        def _():
            rw_sc[7:8, :] = sb_ref[0]
        prev = rw_sc[7:tm + 7, :]
        rw_sc[7:8, :] = rw[tm - 1:tm, :]
        sh_out[0] = rw[tm - 1:tm, :]
    rws = rw + (prev - rw) * mu_ref[...]

    r = rws[:, :rw_dim]
    k = rws[:, rw_dim:2 * rw_dim]
    v = rws[:, 2 * rw_dim:3 * rw_dim]
    lora_in = rws[:, 3 * rw_dim:3 * rw_dim + LANES]
    dg = rws[:, 3 * rw_dim + LANES:]
    z = w0_ref[...] + _dot(jnp.tanh(lora_in).astype(BF16), wl_ref[...])
    nz = -z
    softplus = jnp.maximum(nz, 0.0) + jnp.log(1.0 + jnp.exp(-jnp.abs(nz)))
    w = -softplus - 0.5
    a = _sigmoid(a0_ref[...] + _dot(lora_in.astype(BF16), al_ref[...]))
    g = _dot(_sigmoid(dg).astype(BF16), gl_ref[...])
    kk = k * kk_ref[...]
    norm = jnp.sqrt(_head_sums(kk * kk, ones2_ref[...]))
    kk = kk / jnp.maximum(norm, 1e-12)
    r_out[...] = r
    lw_out[...] = -jnp.exp(w)
    k_out[...] = k * (1.0 + (a - 1.0) * ka_ref[...])
    v_out[...] = v
    kkn_out[...] = kk
    as_out[...] = a
    g_out[...] = g


def mixer_in(x, shift_buf, wts, *, n_seq, seq, pos0, tm, n_heads, q_scale, rw_dim, rope_dim):
    n, d = x.shape
    rw_cols = wts["wrw"].shape[1]
    if tm >= seq:
        assert tm % seq == 0 and n % tm == 0
        nb = tm // seq
        grid = (n // tm, 1)
        rows = seq
        nbg = nb
        row_idx = lambda i, t: (i, 0)
        sb_spec = pl.BlockSpec((nb, 1, rw_cols), lambda i, t: (i, 0, 0))
        q_spec = pl.BlockSpec((nb, n_heads, seq, 2 * LANES), lambda i, t: (i, 0, 0, 0))
        q_shape = (n_seq, n_heads, seq, 2 * LANES)
    else:
        assert seq % tm == 0
        tps = seq // tm
        grid = (n_seq, tps)
        rows = tm
        nbg = 1
        row_idx = lambda i, t: (i * tps + t, 0)
        sb_spec = pl.BlockSpec((1, 1, rw_cols), lambda i, t: (i, 0, 0))
        q_spec = pl.BlockSpec((1, n_heads, tm, 2 * LANES), lambda i, t: (i * tps + t, 0, 0, 0))
        q_shape = (n_seq * tps, n_heads, tm, 2 * LANES)
    rowspec = lambda w: pl.BlockSpec((tm, w), row_idx)
    names = ("wmla", "wrw", "qg", "kvg", "wuqn", "wuqr", "wuk", "invf", "mu", "w0", "wl", "a0",
             "al", "gl", "kk", "ka", "ones2")
    consts = [wts[k] for k in names]
    kern = functools.partial(_mixer_in_kernel, seq=seq, tm=tm, pos0=float(pos0), n_heads=n_heads,
                             q_scale=q_scale, rw_dim=rw_dim, rope_dim=rope_dim)
    f = lambda w, dt=F32: jax.ShapeDtypeStruct((n, w), dt)
    out_shape = (jax.ShapeDtypeStruct(q_shape, BF16), f(2 * LANES, BF16), f(LANES), f(rope_dim)) \
        + (f(rw_dim),) * 7 + (jax.ShapeDtypeStruct((n_seq, 1, rw_cols), F32),)
    out_specs = (q_spec, rowspec(2 * LANES), rowspec(LANES), rowspec(rope_dim)) + (rowspec(rw_dim),) * 7 + (sb_spec,)
    return pl.pallas_call(
        kern, grid=grid,
        in_specs=[rowspec(d), sb_spec] + [_const_spec(c.shape) for c in consts],
        out_specs=out_specs, out_shape=out_shape,
        scratch_shapes=[pltpu.VMEM((tm + 8, rw_cols), F32)] + [pltpu.VMEM((max(tm, seq), LANES), F32)] * 2,
        compiler_params=pltpu.CompilerParams(dimension_semantics=("arbitrary", "arbitrary"),
                                             vmem_limit_bytes=VMEM_LIMIT),
        name="mixer_in",
    )(x, shift_buf, *consts)


def _prompt_attn_kernel(q_ref, qn_ref, kv_ref, o_ref, m_sc, acc_sc, s_sc, *, tq, n_heads):
    qb = pl.program_id(1)
    q = q_ref[0]
    m_sc[...] = jnp.full(m_sc.shape, NEG, F32)
    acc_sc[...] = jnp.zeros(acc_sc.shape, F32)
    ones = jnp.ones((tq, LANES), BF16)
    blk = lambda j: pl.multiple_of(j * tq, tq)

    def scores(qv, j, diagonal):
        s = _dot_nt(qv, kv_ref[0, pl.ds(blk(j), tq), :])
        if diagonal:
            qi = lax.broadcasted_iota(jnp.int32, s.shape, 0) % tq
            kj = lax.broadcasted_iota(jnp.int32, s.shape, 1)
            s = jnp.where(qi >= kj, s, NEG)
        return s

    def consume(s, j):
        m_prev = m_sc[...]
        m_new = jnp.maximum(m_prev, jnp.max(s, axis=-1, keepdims=True))
        corr = jnp.exp2(m_prev - m_new)
        p = jnp.exp2(s - jnp.tile(m_new, (1, tq // LANES))).astype(BF16)
        v_ext = jnp.concatenate([kv_ref[0, pl.ds(blk(j), tq), :LANES], ones], axis=-1)
        acc_sc[...] = acc_sc[...] * jnp.tile(corr, (1, 2)) + _dot(p, v_ext)
        m_sc[...] = m_new

    def step(j, next_scores):
        s = s_sc[...]
        s_next = next_scores()
        consume(s, j)
        s_sc[...] = s_next

    @pl.when(qb == 0)
    def _():
        s_sc[...] = scores(q, 0, True)

    def body(j, carry):
        step(j, lambda: scores(q, j + 1, False))
        return carry

    lax.fori_loop(0, qb - 1, body, 0)

    @pl.when(qb > 0)
    def _():
        step(qb - 1, lambda: scores(q, qb, True))

    step(qb, lambda: scores(qn_ref[0], 0, False))
    acc = acc_sc[...]
    o = (acc[:, :LANES] / acc[:, LANES:]).astype(BF16)
    for h in range(n_heads):
        o_ref[0, :, h * LANES:(h + 1) * LANES] = o[h * tq:(h + 1) * tq]


def prompt_attn(q, kvcat, *, n_seq, seq, tq, n_heads):
    nq = seq // tq
    m = n_heads * tq
    assert seq % tq == 0 and tq % LANES == 0
    kern = functools.partial(_prompt_attn_kernel, tq=tq, n_heads=n_heads)
    return pl.pallas_call(
        kern, grid=(n_seq, nq),
        in_specs=[pl.BlockSpec((1, m, 2 * LANES), lambda b, i: (b * nq + i, 0, 0)),
                  pl.BlockSpec((1, m, 2 * LANES), lambda b, i: (b * nq + jnp.minimum(i + 1, nq - 1), 0, 0)),
                  pl.BlockSpec((1, seq, 2 * LANES), lambda b, i: (b, 0, 0))],
        out_specs=pl.BlockSpec((1, tq, n_heads * LANES), lambda b, i: (b, i, 0)),
        out_shape=jax.ShapeDtypeStruct((n_seq, seq, n_heads * LANES), BF16),
        scratch_shapes=[pltpu.VMEM((m, LANES), F32), pltpu.VMEM((m, 2 * LANES), F32), pltpu.VMEM((m, tq), F32)],
        compiler_params=pltpu.CompilerParams(dimension_semantics=("arbitrary", "arbitrary"),
                                             vmem_limit_bytes=VMEM_LIMIT),
        name="prompt_attn",
    )(q, q, kvcat)


def _softmax_update(s, vals, m_sc, l_sc, acc_sc):
    m_prev = m_sc[...]
    m_new = jnp.maximum(m_prev, jnp.max(s, axis=-1, keepdims=True))
    corr = jnp.exp2(m_prev - m_new)
    p = jnp.exp2(s - m_new)
    l_sc[...] = l_sc[...] * corr + jnp.sum(p, axis=-1, keepdims=True)
    acc_sc[...] = acc_sc[...] * corr + _dot(p.astype(BF16), vals)
    m_sc[...] = m_new


SEQS_PER_STEP = 4


def _sample_attn_kernel(pt_ref, q_ref, kvnew_ref, ckv_hbm, kr_hbm, o_ref, ckv_buf, kr_buf, sem, m_sc, l_sc, acc_sc,
                        s_sc, *, n_pages, tile_pages, rope_dim, dec_seq, n_heads, n_seq):
    i = pl.program_id(0)
    tp = tile_pages
    n_tiles = n_pages // tp

    def page_copies(page, slot, p):
        return (pltpu.make_async_copy(ckv_hbm.at[page], ckv_buf.at[slot, p], sem.at[slot, 0]),
                pltpu.make_async_copy(kr_hbm.at[page], kr_buf.at[slot, p], sem.at[slot, 1]))

    def start_gather(seq, slot):
        def body(pp, carry):
            for parity in range(2):
                p = 2 * pp + parity
                ckv_cp, kr_cp = page_copies(pt_ref[seq * n_pages + p], slot, p)
                ckv_cp.start(priority=parity)
                kr_cp.start(priority=1 - parity)
            return carry
        lax.fori_loop(0, n_pages // 2, body, 0, unroll=4)

    def wait_gather(slot):
        for p in range(n_pages):
            for cp in page_copies(0, slot, p):
                cp.wait()

    def attend(j, slot):
        q = q_ref[j]
        q_lat, q_rope = q[:, :LANES], q[:, LANES:LANES + rope_dim]
        m_sc[...] = jnp.full(m_sc.shape, NEG, F32)
        l_sc[...] = jnp.zeros(l_sc.shape, F32)
        acc_sc[...] = jnp.zeros(acc_sc.shape, F32)

        def vals(t):
            c = ckv_buf[slot, pl.ds(t * tp, tp)]
            return c.reshape(tp * LANES, LANES).astype(BF16)

        def scores(t):
            kr = kr_buf[slot, pl.ds(t * tp, tp)].astype(BF16)
            kr_cat = jnp.concatenate([kr[u] for u in range(tp)], axis=-1)
            return _dot_nt(q_lat, vals(t)) + _dot(q_rope, kr_cat)

        s_sc[...] = scores(0)

        def body(t, carry):
            s = s_sc[...]
            s_next = scores(t + 1)
            _softmax_update(s, vals(t), m_sc, l_sc, acc_sc)
            s_sc[...] = s_next
            return carry

        lax.fori_loop(0, n_tiles - 1, body, 0)
        _softmax_update(s_sc[...], vals(n_tiles - 1), m_sc, l_sc, acc_sc)
        kn = kvnew_ref[j]
        s = _dot_nt(q, kn)
        qi = lax.broadcasted_iota(jnp.int32, s.shape, 0) % dec_seq
        kj = lax.broadcasted_iota(jnp.int32, s.shape, 1)
        _softmax_update(jnp.where(qi >= kj, s, NEG), kn[:, :LANES], m_sc, l_sc, acc_sc)
        o = (acc_sc[...] / l_sc[...]).astype(BF16)
        for h in range(n_heads):
            o_ref[j, :, h * LANES:(h + 1) * LANES] = o[h * dec_seq:(h + 1) * dec_seq]

    first = SEQS_PER_STEP * i

    @pl.when(i == 0)
    def _():
        for j in range(SEQS_PER_STEP):
            start_gather(j, j)

    for j in range(SEQS_PER_STEP):
        wait_gather(j)
        attend(j, j)

        @pl.when(first + SEQS_PER_STEP + j < n_seq)
        def _():
            start_gather(first + SEQS_PER_STEP + j, j)


def sample_attn(q, kvnew, cache_ckv, cache_kr_t, page_table, *, dec_seq, n_heads, tile_pages):
    b, n_pages = page_table.shape
    page = cache_ckv.shape[1]
    rope_dim = cache_kr_t.shape[1]
    assert n_pages % tile_pages == 0 and n_pages % 2 == 0 and page == LANES and b % SEQS_PER_STEP == 0
    m = n_heads * dec_seq
    kern = functools.partial(_sample_attn_kernel, n_pages=n_pages, tile_pages=tile_pages, rope_dim=rope_dim,
                             dec_seq=dec_seq, n_heads=n_heads, n_seq=b)
    per_step = lambda rows, width: pl.BlockSpec((SEQS_PER_STEP, rows, width), lambda i, pt: (i, 0, 0))
    grid_spec = pltpu.PrefetchScalarGridSpec(
        num_scalar_prefetch=1, grid=(b // SEQS_PER_STEP,),
        in_specs=[per_step(m, 2 * LANES), per_step(dec_seq, 2 * LANES),
                  pl.BlockSpec(memory_space=pl.ANY), pl.BlockSpec(memory_space=pl.ANY)],
        out_specs=per_step(dec_seq, n_heads * LANES),
        scratch_shapes=[pltpu.VMEM((SEQS_PER_STEP, n_pages, page, LANES), F32),
                        pltpu.VMEM((SEQS_PER_STEP, n_pages, rope_dim, page), F32),
                        pltpu.SemaphoreType.DMA((SEQS_PER_STEP, 2)),
                        pltpu.VMEM((m, 1), F32), pltpu.VMEM((m, 1), F32), pltpu.VMEM((m, LANES), F32),
                        pltpu.VMEM((m, tile_pages * page), F32)])
    return pl.pallas_call(
        kern, grid_spec=grid_spec,
        out_shape=jax.ShapeDtypeStruct((b, dec_seq, n_heads * LANES), BF16),
        compiler_params=pltpu.CompilerParams(dimension_semantics=("arbitrary",), vmem_limit_bytes=VMEM_LIMIT),
        name="sample_attn",
    )(page_table.reshape(-1), q, kvnew, cache_ckv, cache_kr_t)


def _cumsum_rows(tri, x):
    if x.shape[0] < 16:
        return jnp.dot(tri.astype(F32), x, preferred_element_type=F32, precision=lax.Precision.HIGHEST)
    hi = x.astype(BF16)
    lo = (x - hi.astype(F32)).astype(BF16)
    both = _dot(tri, jnp.concatenate([hi, lo], axis=-1))
    return both[:, :x.shape[1]] + both[:, x.shape[1]:]


def _rwkv_chunk(rs, lws, ks, vs, kks, asigs, s0s, tri, strict, incl, head0):
    n = range(len(rs))
    c = rs[0].shape[0]
    c2 = 2 * c

    def stack(x):
        return jnp.concatenate([jnp.where(head0, x, 0.0), jnp.where(head0, 0.0, x)], axis=0)

    cs = [_cumsum_rows(tri, lws[i]) for i in n]
    cs_end = [cs[i][c - 1:c, :] for i in n]
    e_neg = [jnp.exp(-cs[i]) for i in n]
    e_end = [jnp.exp(cs_end[i] - cs[i]) for i in n]
    bvec = [kks[i] * asigs[i] for i in n]
    a_t = [stack(-kks[i] * jnp.exp(cs[i] - lws[i])) for i in n]
    r_t = [stack(rs[i] * jnp.exp(cs[i])).astype(BF16) for i in n]
    k_t = [stack(ks[i] * e_neg[i]) for i in n]
    b_t = [stack(bvec[i] * e_neg[i]) for i in n]
    k_e = [stack(ks[i] * e_end[i]).astype(BF16) for i in n]
    b_e = [stack(bvec[i] * e_end[i]).astype(BF16) for i in n]
    v_s = [stack(vs[i]).astype(BF16) for i in n]

    sc = [_dot_nt(jnp.concatenate([a_t[i].astype(BF16), r_t[i]], axis=0),
                  jnp.concatenate([k_t[i], b_t[i]], axis=0).astype(BF16)) for i in n]
    l_ak = [jnp.where(strict, sc[i][:c2, :c2], 0.0).astype(BF16) for i in n]
    lp = [jnp.where(strict, sc[i][:c2, c2:], 0.0).astype(BF16) for i in n]
    a_rk = [jnp.where(incl, sc[i][c2:, :c2], 0.0).astype(BF16) for i in n]
    a_rb = [jnp.where(incl, sc[i][c2:, c2:], 0.0).astype(BF16) for i in n]

    x = [jnp.concatenate([a_t[i], _dot(l_ak[i], v_s[i])], axis=-1) for i in n]
    n_fac = int(math.log2(c))
    for f in range(n_fac):
        x = [x[i] + _dot(lp[i], x[i].astype(BF16)) for i in n]
        if f + 1 < n_fac:
            lp = [_dot(lp[i], lp[i]).astype(BF16) for i in n]

    s0b = [s0s[i].astype(BF16) for i in n]
    ar = [_dot_nt(jnp.concatenate([x[i][:, :LANES].astype(BF16), r_t[i]], axis=0), s0b[i]) for i in n]
    u = [(ar[i][:c2] + x[i][:, LANES:]).astype(BF16) for i in n]
    vu = [jnp.concatenate([v_s[i], u[i]], axis=0) for i in n]
    y_st = [ar[i][c2:] + _dot(jnp.concatenate([a_rk[i], a_rb[i]], axis=-1), vu[i]) for i in n]
    ys = [y_st[i][:c] + y_st[i][c:] for i in n]
    s_new = [s0s[i] * jnp.exp(cs_end[i]) + _dot_tn(vu[i], jnp.concatenate([k_e[i], b_e[i]], axis=0)) for i in n]
    return ys, s_new


def _rwkv_kernel(r_ref, lw_ref, k_ref, v_ref, kk_ref, as_ref, g_ref, s0_ref, rk_ref, lg_ref, lb_ref, ones2_ref,
                 y_ref, sout_ref, s_sc, *, chunk, n_pairs, n_seq_blk):
    ci = pl.program_id(1)

    @pl.when(ci == 0)
    def _():
        s_sc[...] = s0_ref[...]

    c = chunk
    ti = lax.broadcasted_iota(jnp.int32, (c, c), 0)
    tj = lax.broadcasted_iota(jnp.int32, (c, c), 1)
    tri = (ti >= tj).astype(BF16)
    si = lax.broadcasted_iota(jnp.int32, (2 * c, 2 * c), 0)
    sj = lax.broadcasted_iota(jnp.int32, (2 * c, 2 * c), 1)
    strict = si > sj
    incl = si >= sj
    head0 = lax.broadcasted_iota(jnp.int32, (c, LANES), 1) < (LANES // HEAD_PAIR)
    ones2 = ones2_ref[...]
    inv_n = 1.0 / (LANES // HEAD_PAIR)

    streams = [(b, p) for b in range(n_seq_blk) for p in range(n_pairs)]
    n = range(len(streams))
    lanes = [slice(p * LANES, (p + 1) * LANES) for _, p in streams]
    get = lambda ref: [ref[b, :, lanes[i]] for i, (b, _) in enumerate(streams)]
    par = lambda ref: [ref[:, lanes[i]] for i in n]
    rs, ks, vs = get(r_ref), get(k_ref), get(v_ref)
    ys, s_new = _rwkv_chunk(rs, get(lw_ref), ks, vs, get(kk_ref), get(as_ref),
                            [s_sc[b, p] for b, p in streams], tri, strict, incl, head0)
    for i, (b, p) in enumerate(streams):
        s_sc[b, p] = s_new[i]
    lg, lb, rk = par(lg_ref), par(lb_ref), par(rk_ref)
    mu = [_head_sums(ys[i], ones2) * inv_n for i in n]
    d = [ys[i] - mu[i] for i in n]
    var = [_head_sums(d[i] * d[i], ones2) * inv_n for i in n]
    bonus = [_head_sums(rs[i] * ks[i] * rk[i], ones2) * vs[i] for i in n]
    gs = get(g_ref)
    for i, (b, p) in enumerate(streams):
        o = d[i] * lax.rsqrt(var[i] + GN_EPS) * lg[i] + lb[i] + bonus[i]
        y_ref[b, :, lanes[i]] = (o * gs[i]).astype(BF16)

    @pl.when(ci == pl.num_programs(1) - 1)
    def _():
        sout_ref[...] = s_sc[...]


def rwkv_chunked(r, lw, k, v, kk, asig, g, s0, rk, lg, lb, ones2, *, chunk, n_seq_blk):
    n_seq, seq, rw_dim = r.shape
    n_pairs = rw_dim // LANES
    nc = seq // chunk
    assert n_seq % n_seq_blk == 0 and seq % chunk == 0
    kern = functools.partial(_rwkv_kernel, chunk=chunk, n_pairs=n_pairs, n_seq_blk=n_seq_blk)
    row = pl.BlockSpec((n_seq_blk, chunk, rw_dim), lambda b, c: (b, c, 0))
    st = pl.BlockSpec((n_seq_blk, n_pairs, LANES, LANES), lambda b, c: (b, 0, 0, 0))
    return pl.pallas_call(
        kern, grid=(n_seq // n_seq_blk, nc),
        in_specs=[row] * 7 + [st] + [_const_spec(x.shape) for x in (rk, lg, lb, ones2)],
        out_specs=(row, st),
        out_shape=(jax.ShapeDtypeStruct((n_seq, seq, rw_dim), BF16), jax.ShapeDtypeStruct(s0.shape, F32)),
        scratch_shapes=[pltpu.VMEM((n_seq_blk, n_pairs, LANES, LANES), F32)],
        compiler_params=pltpu.CompilerParams(dimension_semantics=("arbitrary", "arbitrary"),
                                             vmem_limit_bytes=VMEM_LIMIT),
        name="rwkv_chunked",
    )(r, lw, k, v, kk, asig, g, s0, rk, lg, lb, ones2)


def _block_diag(blocks):
    n, r, c = blocks.shape
    eye = jnp.eye(n, dtype=blocks.dtype)
    return (eye[:, None, :, None] * blocks[:, :, None, :]).reshape(n * r, n * c)


def _rope_group(w, rope_dim):
    half = rope_dim // 2
    x1, x2 = w[..., :half], w[..., half:]
    pad = jnp.zeros(w.shape[:-1] + (LANES - 2 * rope_dim,), w.dtype)
    return jnp.concatenate([x1, x2, -x2, x1, pad], axis=-1)


def _prep_weights(p, dims):
    q_rank, kv_rank, rope_dim, n_heads, nope, rw_dim = (dims[k] for k in
                                                        ("q_rank", "kv_rank", "rope_dim", "n_heads", "nope", "rw_dim"))
    w_in = p["w_in"]
    o1, o2 = q_rank + kv_rank, q_rank + kv_rank + rope_dim
    w_uq = p["w_uq"]
    row = lambda a: a.reshape(1, -1).astype(F32)
    d_wl = p["w_lora_up"].shape[0]
    d_al = p["a_lora_up"].shape[0]
    assert d_wl + d_al == LANES
    half = jnp.arange(0, rope_dim, 2, dtype=F32)
    inv = ROPE_BASE ** (-half / rope_dim)
    invf = jnp.concatenate([inv, inv, jnp.zeros((LANES - rope_dim,), F32)]).reshape(1, LANES)
    rw_head = rw_dim // dims["rw_heads"]
    assert HEAD_PAIR * rw_head == LANES and HEAD_PAIR * nope == LANES and kv_rank == LANES
    pair_ones = _block_diag(jnp.ones((HEAD_PAIR, rw_head, rw_head), F32))
    w_out = p["w_out"]
    d_mla = n_heads * dims["mla_v"]
    wuk = jnp.transpose(p["w_uk"], (1, 2, 0)).reshape(n_heads // HEAD_PAIR, HEAD_PAIR, nope, kv_rank)
    wuv = jnp.transpose(p["w_uv"], (1, 0, 2)).reshape(n_heads // HEAD_PAIR, HEAD_PAIR, kv_rank, dims["mla_v"])
    return {
        "wmla": jnp.concatenate([w_in[:, :o1], _rope_group(w_in[:, o1:o2], rope_dim)], axis=1).astype(BF16),
        "wrw": w_in[:, o2:].astype(BF16),
        "qg": row(p["q_norm_g"]), "kvg": row(p["kv_norm_g"]),
        "wuqn": w_uq[:, :, :nope].reshape(q_rank, n_heads * nope).astype(BF16),
        "wuqr": _rope_group(w_uq[:, :, nope:], rope_dim).reshape(q_rank, n_heads * LANES).astype(BF16),
        "wuk": jnp.stack([_block_diag(w) for w in wuk]).astype(BF16),
        "invf": invf,
        "mu": row(p["shift_mu"]), "w0": row(p["w0"]), "a0": row(p["a0"]),
        "wl": jnp.concatenate([p["w_lora_up"], jnp.zeros((d_al, rw_dim), F32)], axis=0).astype(BF16),
        "al": jnp.concatenate([jnp.zeros((d_wl, rw_dim), F32), p["a_lora_up"]], axis=0).astype(BF16),
        "gl": p["g_lora_up"].astype(BF16),
        "kk": row(p["k_k"]), "ka": row(p["k_a"]),
        "ones2": jnp.concatenate([pair_ones, pair_ones], axis=0).astype(BF16),
        "rk": row(p["r_k"]), "lg": row(p["lnx_g"]), "lb": row(p["lnx_b"]),
        "wuv": jnp.stack([_block_diag(w) for w in wuv]).astype(BF16),
        "woa": w_out[:d_mla].astype(BF16), "wob": w_out[d_mla:].astype(BF16),
    }


def _pair_state(s):
    b, h, n, _ = s.shape
    s = s.reshape(b, h // HEAD_PAIR, HEAD_PAIR, n, n)
    eye = jnp.eye(HEAD_PAIR, dtype=s.dtype)
    return (s[:, :, :, :, None, :] * eye[None, None, :, None, :, None]).reshape(b, h // HEAD_PAIR, HEAD_PAIR * n,
                                                                                 HEAD_PAIR * n)


def _unpair_state(s, n):
    b, hp = s.shape[:2]
    s = s.reshape(b, hp, HEAD_PAIR, n, HEAD_PAIR, n)
    return jnp.stack([s[:, :, i, :, i, :] for i in range(HEAD_PAIR)], axis=2).reshape(b, hp * HEAD_PAIR, n, n)


def _group(x, shift_buf, wkv0, wts, ffa, ffb, lns, dims, *, pos0, attend, tm_in, chunk, rw_seq_blk, alpha):
    n_seq, seq, d = x.shape
    n = n_seq * seq
    n_heads, rw_dim, rope_dim = dims["n_heads"], dims["rw_dim"], dims["rope_dim"]
    x1 = ffn_ln(x.reshape(n, d), *ffa, lns["ln1_g"], lns["ln1_b"], alpha=alpha)
    (q, kvcat, ckv, kr, r, lw, k, v, kk, asig, g, shift_out) = mixer_in(
        x1, shift_buf.reshape(n_seq, 1, -1), wts, n_seq=n_seq, seq=seq, pos0=pos0, tm=tm_in, n_heads=n_heads,
        q_scale=dims["q_scale"], rw_dim=rw_dim, rope_dim=rope_dim)
    ocat = attend(q.reshape(q.shape[0], n_heads * q.shape[2], q.shape[3]), kvcat.reshape(n_seq, seq, -1))
    seqs = lambda a: a.reshape(n_seq, seq, -1)
    yrw, s_out = rwkv_chunked(seqs(r), seqs(lw), seqs(k), seqs(v), seqs(kk), seqs(asig), seqs(g), _pair_state(wkv0),
                              wts["rk"], wts["lg"], wts["lb"], wts["ones2"], chunk=chunk, n_seq_blk=rw_seq_blk)
    y = out_ffn_ln(x1, ocat.reshape(n, -1), yrw.reshape(n, -1), wts["wuv"], wts["woa"], wts["wob"], lns["ln2_g"], lns["ln2_b"],
                   *ffb, lns["ln3_g"], lns["ln3_b"], alpha=alpha)
    return (y.reshape(n_seq, seq, d), ckv.reshape(1, n_seq, seq, -1), kr.reshape(1, n_seq, seq, -1),
            _unpair_state(s_out, rw_dim // dims["rw_heads"])[None], shift_out.reshape(1, n_seq, -1))


def kernel(x_prompt, x_sample, cache_ckv, cache_krope, state_wkv, state_shift, page_table, ln1_g, ln1_b, ffa_w_gate, ffa_w_up, ffa_w_down, w_in, q_norm_g, w_uq, kv_norm_g, w_uk, w_uv, shift_mu, w0, w_lora_up, a0, a_lora_up, g_lora_up, k_k, k_a, r_k, lnx_g, lnx_b, w_out, ln2_g, ln2_b, ffb_w_gate, ffb_w_up, ffb_w_down, ln3_g, ln3_b):
    depth = w_in.shape[0]
    assert depth == 1
    alpha = (2 * depth) ** 0.25
    n_heads, nope_rope = w_uq.shape[2], w_uq.shape[3]
    rope_dim = cache_krope.shape[-1]
    rw_heads, rw_head = state_wkv.shape[2], state_wkv.shape[3]
    dims = dict(q_rank=w_uq.shape[1], kv_rank=w_uk.shape[1], rope_dim=rope_dim, n_heads=n_heads,
                nope=nope_rope - rope_dim, rw_dim=rw_heads * rw_head, rw_heads=rw_heads, mla_v=w_uv.shape[3],
                q_scale=float(nope_rope) ** -0.5 * math.log2(math.e))
    p = dict(w_in=w_in[0], q_norm_g=q_norm_g[0], w_uq=w_uq[0], kv_norm_g=kv_norm_g[0], w_uk=w_uk[0], w_uv=w_uv[0],
             shift_mu=shift_mu[0], w0=w0[0], w_lora_up=w_lora_up[0], a0=a0[0], a_lora_up=a_lora_up[0],
             g_lora_up=g_lora_up[0], k_k=k_k[0], k_a=k_a[0], r_k=r_k[0], lnx_g=lnx_g[0], lnx_b=lnx_b[0],
             w_out=w_out[0])
    wts = _prep_weights(p, dims)
    row = lambda a: a[0].reshape(1, -1)
    lns = dict(ln1_g=row(ln1_g), ln1_b=row(ln1_b), ln2_g=row(ln2_g), ln2_b=row(ln2_b), ln3_g=row(ln3_g),
               ln3_b=row(ln3_b))
    ffa = (ffa_w_gate[0].astype(BF16), ffa_w_up[0].astype(BF16), ffa_w_down[0].astype(BF16))
    ffb = (ffb_w_gate[0].astype(BF16), ffb_w_up[0].astype(BF16), ffb_w_down[0].astype(BF16))

    bp, sp, _ = x_prompt.shape
    bs, ss, _ = x_sample.shape
    n_pages, page = page_table.shape[1], cache_ckv.shape[2]
    past_len = n_pages * page
    tq = min(512, sp)

    attend_p = functools.partial(prompt_attn, n_seq=bp, seq=sp, tq=tq, n_heads=n_heads)
    out_p = _group(x_prompt, jnp.zeros((bp, state_shift.shape[-1]), F32),
                   jnp.zeros((bp,) + state_wkv.shape[2:], F32), wts, ffa, ffb, lns, dims,
                   pos0=0.0, attend=attend_p, tm_in=tq, chunk=min(64, sp), rw_seq_blk=math.gcd(bp, 4), alpha=alpha)

    kr_t = jnp.swapaxes(cache_krope[0], 1, 2)
    attend_s = lambda q, kvnew: sample_attn(q, kvnew, cache_ckv[0], kr_t, page_table, dec_seq=ss,
                                            n_heads=n_heads, tile_pages=math.gcd(n_pages, 32))
    out_s = _group(x_sample, state_shift[0], state_wkv[0], wts, ffa, ffb, lns, dims,
                   pos0=float(past_len), attend=attend_s, tm_in=min(128, bs * ss), chunk=ss,
                   rw_seq_blk=math.gcd(bs, 8), alpha=alpha)
    return (out_p[0], out_s[0]) + out_p[1:] + out_s[1:]
```

```python
import functools
import math

import jax
import jax.numpy as jnp
from jax import lax
from jax.experimental import pallas as pl
from jax.experimental.pallas import tpu as pltpu

F32 = jnp.float32
BF16 = jnp.bfloat16

LANES = 128
HEAD_PAIR = 2
LN_EPS = 1e-5
RMS_EPS = 1e-6
GN_EPS = 64e-5
ROPE_BASE = 10000.0
NEG = -1e30
VMEM_LIMIT = 56 * 1024 * 1024


def _dot(a, b):
    return jnp.dot(a, b, preferred_element_type=F32)


def _dot_nt(a, b):
    return lax.dot_general(a, b, (((1,), (1,)), ((), ())), preferred_element_type=F32)


def _dot_tn(a, b):
    return lax.dot_general(a, b, (((0,), (0,)), ((), ())), preferred_element_type=F32)


def _layernorm(y, g, b):
    mu = jnp.mean(y, axis=-1, keepdims=True)
    d = y - mu
    var = jnp.mean(d * d, axis=-1, keepdims=True)
    return d * lax.rsqrt(var + LN_EPS) * g + b


def _rmsnorm(y, g):
    return y * lax.rsqrt(jnp.mean(y * y, axis=-1, keepdims=True) + RMS_EPS) * g


def _sigmoid(x):
    return 1.0 / (1.0 + jnp.exp(-x))


def _head_sums(x, ones2):
    hi = x.astype(BF16)
    lo = (x - hi.astype(F32)).astype(BF16)
    groups = [slice(g * LANES, (g + 1) * LANES) for g in range(x.shape[1] // LANES)]
    return jnp.concatenate([_dot(jnp.concatenate([hi[:, g], lo[:, g]], axis=-1), ones2) for g in groups], axis=-1)


def _const_spec(shape):
    nd = len(shape)
    return pl.BlockSpec(shape, lambda *_: (0,) * nd, pipeline_mode=pl.Buffered(1))


def _swiglu_ln(x, wg_ref, wu_ref, wd_ref, g_ref, b_ref, alpha, ff_chunk):
    xb = x.astype(BF16)
    d_ff = wg_ref.shape[1]
    acc = jnp.zeros(x.shape, F32)
    for c in range(d_ff // ff_chunk):
        sl = slice(c * ff_chunk, (c + 1) * ff_chunk)
        gate = _dot(xb, wg_ref[:, sl])
        up = _dot(xb, wu_ref[:, sl])
        h = (gate * _sigmoid(gate) * up).astype(BF16)
        acc = acc + _dot(h, wd_ref[sl, :])
    return _layernorm(alpha * x + 0.5 * acc, g_ref[...], b_ref[...])


def _ffn_ln_kernel(x_ref, wg_ref, wu_ref, wd_ref, g_ref, b_ref, o_ref, *, alpha, ff_chunk):
    o_ref[...] = _swiglu_ln(x_ref[...], wg_ref, wu_ref, wd_ref, g_ref, b_ref, alpha, ff_chunk)


def _out_ffn_ln_kernel(x_ref, ocat_ref, yrw_ref, wuv_ref, woa_ref, wob_ref, g2_ref, b2_ref,
                       wg_ref, wu_ref, wd_ref, g3_ref, b3_ref, o_ref, *, alpha, ff_chunk):
    pair_w = wuv_ref.shape[1]
    y_mla = jnp.concatenate([_dot(ocat_ref[:, hp * pair_w:(hp + 1) * pair_w], wuv_ref[hp])
                             for hp in range(wuv_ref.shape[0])], axis=-1).astype(BF16)
    y = _dot(y_mla, woa_ref[...]) + _dot(yrw_ref[...], wob_ref[...])
    x2 = _layernorm(alpha * x_ref[...] + y, g2_ref[...], b2_ref[...])
    o_ref[...] = _swiglu_ln(x2, wg_ref, wu_ref, wd_ref, g3_ref, b3_ref, alpha, ff_chunk)


def _row_tile(n, want):
    t = min(n, want)
    assert n % t == 0
    return t


def _ffn_chunk(d_ff):
    for c in (512, 256, 128):
        if d_ff % c == 0:
            return c
    return d_ff


def ffn_ln(x, wg, wu, wd, g, b, *, alpha, tm=512):
    n, d = x.shape
    tm = _row_tile(n, tm)
    kern = functools.partial(_ffn_ln_kernel, alpha=alpha, ff_chunk=_ffn_chunk(wg.shape[1]))
    row = pl.BlockSpec((tm, d), lambda i: (i, 0))
    return pl.pallas_call(
        kern, grid=(n // tm,),
        in_specs=[row, _const_spec(wg.shape), _const_spec(wu.shape), _const_spec(wd.shape),
                  _const_spec(g.shape), _const_spec(b.shape)],
        out_specs=row, out_shape=jax.ShapeDtypeStruct((n, d), F32),
        compiler_params=pltpu.CompilerParams(dimension_semantics=("arbitrary",), vmem_limit_bytes=VMEM_LIMIT),
        name="ffn_ln",
    )(x, wg, wu, wd, g, b)


def out_ffn_ln(x, ocat, yrw, wuv, woa, wob, g2, b2, wg, wu, wd, g3, b3, *, alpha, tm=512):
    n, d = x.shape
    tm = _row_tile(n, tm)
    kern = functools.partial(_out_ffn_ln_kernel, alpha=alpha, ff_chunk=_ffn_chunk(wg.shape[1]))
    row = lambda w: pl.BlockSpec((tm, w), lambda i: (i, 0))
    consts = (wuv, woa, wob, g2, b2, wg, wu, wd, g3, b3)
    return pl.pallas_call(
        kern, grid=(n // tm,),
        in_specs=[row(d), row(ocat.shape[1]), row(yrw.shape[1])] + [_const_spec(c.shape) for c in consts],
        out_specs=row(d), out_shape=jax.ShapeDtypeStruct((n, d), F32),
        compiler_params=pltpu.CompilerParams(dimension_semantics=("arbitrary",), vmem_limit_bytes=VMEM_LIMIT),
        name="out_ffn_ln",
    )(x, ocat, yrw, *consts)


def _mixer_in_kernel(x_ref, sb_ref, wmla_ref, wrw_ref, qg_ref, kvg_ref, wuqn_ref, wuqr_ref,
                     wuk_ref, invf_ref, mu_ref, w0_ref, wl_ref, a0_ref, al_ref, gl_ref, kk_ref, ka_ref,
                     ones2_ref,
                     q_out, kvcat_out, ckv_out, kr_out, r_out, lw_out, k_out, v_out, kkn_out, as_out, g_out,
                     sh_out, rw_sc, cos_sc, sin_sc, *, seq, tm, pos0, n_heads, q_scale, rw_dim, rope_dim):
    t = pl.program_id(1)
    xb = x_ref[...].astype(BF16)
    whole_seqs = tm >= seq

    row = lax.broadcasted_iota(jnp.int32, (tm, 1), 0)
    table_rows = slice(None) if whole_seqs else pl.ds(pl.multiple_of(t * tm, tm), tm)

    @pl.when(pl.program_id(0) == 0)
    def _():
        row_in_seq = (row % seq) if whole_seqs else (row + t * tm)
        ang = (row_in_seq.astype(F32) + pos0) * invf_ref[...]
        lane = lax.broadcasted_iota(jnp.int32, (tm, LANES), 1)
        cos_sc[table_rows, :] = jnp.where(lane < rope_dim, jnp.cos(ang), 0.0)
        sin_sc[table_rows, :] = jnp.where(lane < rope_dim, jnp.sin(ang), 0.0)

    cosm = cos_sc[table_rows, :]
    sinm = sin_sc[table_rows, :]

    def rope(grp):
        return grp * cosm + pltpu.roll(grp, LANES - rope_dim, axis=1) * sinm

    mla = _dot(xb, wmla_ref[...])
    q_rank = qg_ref.shape[1]
    ckv = _rmsnorm(mla[:, q_rank:q_rank + LANES], kvg_ref[...])
    kr = rope(mla[:, q_rank + LANES:])
    ckv_out[...] = ckv
    kr_out[...] = kr[:, :rope_dim]
    kvcat_out[:, :LANES] = ckv.astype(BF16)
    kvcat_out[:, LANES:] = kr.astype(BF16)

    cq = _rmsnorm(mla[:, :q_rank], qg_ref[...]).astype(BF16)
    q_nope = _dot(cq, wuqn_ref[...]).astype(BF16)
    q_rg = _dot(cq, wuqr_ref[...])
    nbg, rows = q_out.shape[0], q_out.shape[2]
    for hp in range(n_heads // HEAD_PAIR):
        q_lat = _dot(q_nope[:, hp * LANES:(hp + 1) * LANES], wuk_ref[hp])
        for j in range(HEAD_PAIR):
            h = hp * HEAD_PAIR + j
            q_h = jnp.concatenate([q_lat[:, j * LANES:(j + 1) * LANES] * q_scale,
                                   rope(q_rg[:, h * LANES:(h + 1) * LANES]) * q_scale], axis=-1).astype(BF16)
            q_out[:, h] = q_h.reshape(nbg, rows, 2 * LANES)

    rw = _dot(xb, wrw_ref[...])
    rw_sc[8:tm + 8, :] = rw
    if whole_seqs:
        nb = tm // seq
        rw_sc[7:8, :] = jnp.zeros((1, rw.shape[1]), F32)
        prev = rw_sc[7:tm + 7, :]
        first = jnp.broadcast_to(sb_ref[...], (nb, seq, rw.shape[1])).reshape(tm, rw.shape[1])
        prev = jnp.where(row % seq == 0, first, prev)
        sh_out[...] = rw.reshape(nb, seq, rw.shape[1])[:, seq - 1:seq, :]
    else:
        @pl.when(t == 0)
        def _():
            rw_sc[7:8, :] = sb_ref[0]
        prev = rw_sc[7:tm + 7, :]
        rw_sc[7:8, :] = rw[tm - 1:tm, :]
        sh_out[0] = rw[tm - 1:tm, :]
    rws = rw + (prev - rw) * mu_ref[...]

    r = rws[:, :rw_dim]
    k = rws[:, rw_dim:2 * rw_dim]
    v = rws[:, 2 * rw_dim:3 * rw_dim]
    lora_in = rws[:, 3 * rw_dim:3 * rw_dim + LANES]
    dg = rws[:, 3 * rw_dim + LANES:]
    z = w0_ref[...] + _dot(jnp.tanh(lora_in).astype(BF16), wl_ref[...])
    nz = -z
    softplus = jnp.maximum(nz, 0.0) + jnp.log(1.0 + jnp.exp(-jnp.abs(nz)))
    w = -softplus - 0.5
    a = _sigmoid(a0_ref[...] + _dot(lora_in.astype(BF16), al_ref[...]))
    g = _dot(_sigmoid(dg).astype(BF16), gl_ref[...])
    kk = k * kk_ref[...]
    norm = jnp.sqrt(_head_sums(kk * kk, ones2_ref[...]))
    kk = kk / jnp.maximum(norm, 1e-12)
    r_out[...] = r
    lw_out[...] = -jnp.exp(w)
    k_out[...] = k * (1.0 + (a - 1.0) * ka_ref[...])
    v_out[...] = v
    kkn_out[...] = kk
    as_out[...] = a
    g_out[...] = g


def mixer_in(x, shift_buf, wts, *, n_seq, seq, pos0, tm, n_heads, q_scale, rw_dim, rope_dim):
    n, d = x.shape
    rw_cols = wts["wrw"].shape[1]
    if tm >= seq:
        assert tm % seq == 0 and n % tm == 0
        nb = tm // seq
        grid = (n // tm, 1)
        rows = seq
        nbg = nb
        row_idx = lambda i, t: (i, 0)
        sb_spec = pl.BlockSpec((nb, 1, rw_cols), lambda i, t: (i, 0, 0))
        q_spec = pl.BlockSpec((nb, n_heads, seq, 2 * LANES), lambda i, t: (i, 0, 0, 0))
        q_shape = (n_seq, n_heads, seq, 2 * LANES)
    else:
        assert seq % tm == 0
        tps = seq // tm
        grid = (n_seq, tps)
        rows = tm
        nbg = 1
        row_idx = lambda i, t: (i * tps + t, 0)
        sb_spec = pl.BlockSpec((1, 1, rw_cols), lambda i, t: (i, 0, 0))
        q_spec = pl.BlockSpec((1, n_heads, tm, 2 * LANES), lambda i, t: (i * tps + t, 0, 0, 0))
        q_shape = (n_seq * tps, n_heads, tm, 2 * LANES)
    rowspec = lambda w: pl.BlockSpec((tm, w), row_idx)
    names = ("wmla", "wrw", "qg", "kvg", "wuqn", "wuqr", "wuk", "invf", "mu", "w0", "wl", "a0",
             "al", "gl", "kk", "ka", "ones2")
    consts = [wts[k] for k in names]
    kern = functools.partial(_mixer_in_kernel, seq=seq, tm=tm, pos0=float(pos0), n_heads=n_heads,
                             q_scale=q_scale, rw_dim=rw_dim, rope_dim=rope_dim)
    f = lambda w, dt=F32: jax.ShapeDtypeStruct((n, w), dt)
    out_shape = (jax.ShapeDtypeStruct(q_shape, BF16), f(2 * LANES, BF16), f(LANES), f(rope_dim)) \
        + (f(rw_dim),) * 7 + (jax.ShapeDtypeStruct((n_seq, 1, rw_cols), F32),)
    out_specs = (q_spec, rowspec(2 * LANES), rowspec(LANES), rowspec(rope_dim)) + (rowspec(rw_dim),) * 7 + (sb_spec,)
    return pl.pallas_call(
        kern, grid=grid,
        in_specs=[rowspec(d), sb_spec] + [_const_spec(c.shape) for c in consts],
        out_specs=out_specs, out_shape=out_shape,
        scratch_shapes=[pltpu.VMEM((tm + 8, rw_cols), F32)] + [pltpu.VMEM((max(tm, seq), LANES), F32)] * 2,
        compiler_params=pltpu.CompilerParams(dimension_semantics=("arbitrary", "arbitrary"),
                                             vmem_limit_bytes=VMEM_LIMIT),
        name="mixer_in",
    )(x, shift_buf, *consts)


def _prompt_attn_kernel(q_ref, qn_ref, kv_ref, o_ref, m_sc, acc_sc, s_sc, *, tq, n_heads):
    qb = pl.program_id(1)
    q = q_ref[0]
    m_sc[...] = jnp.full(m_sc.shape, NEG, F32)
    acc_sc[...] = jnp.zeros(acc_sc.shape, F32)
    ones = jnp.ones((tq, LANES), BF16)
    blk = lambda j: pl.multiple_of(j * tq, tq)

    def scores(qv, j, diagonal):
        s = _dot_nt(qv, kv_ref[0, pl.ds(blk(j), tq), :])
        if diagonal:
            qi = lax.broadcasted_iota(jnp.int32, s.shape, 0) % tq
            kj = lax.broadcasted_iota(jnp.int32, s.shape, 1)
            s = jnp.where(qi >= kj, s, NEG)
        return s

    def consume(s, j):
        m_prev = m_sc[...]
        m_new = jnp.maximum(m_prev, jnp.max(s, axis=-1, keepdims=True))
        corr = jnp.exp2(m_prev - m_new)
        p = jnp.exp2(s - jnp.tile(m_new, (1, tq // LANES))).astype(BF16)
        v_ext = jnp.concatenate([kv_ref[0, pl.ds(blk(j), tq), :LANES], ones], axis=-1)
        acc_sc[...] = acc_sc[...] * jnp.tile(corr, (1, 2)) + _dot(p, v_ext)
        m_sc[...] = m_new

    def step(j, next_scores):
        s = s_sc[...]
        s_next = next_scores()
        consume(s, j)
        s_sc[...] = s_next

    @pl.when(qb == 0)
    def _():
        s_sc[...] = scores(q, 0, True)

    def body(j, carry):
        step(j, lambda: scores(q, j + 1, False))
        return carry

    lax.fori_loop(0, qb - 1, body, 0)

    @pl.when(qb > 0)
    def _():
        step(qb - 1, lambda: scores(q, qb, True))

    step(qb, lambda: scores(qn_ref[0], 0, False))
    acc = acc_sc[...]
    o = (acc[:, :LANES] / acc[:, LANES:]).astype(BF16)
    for h in range(n_heads):
        o_ref[0, :, h * LANES:(h + 1) * LANES] = o[h * tq:(h + 1) * tq]


def prompt_attn(q, kvcat, *, n_seq, seq, tq, n_heads):
    nq = seq // tq
    m = n_heads * tq
    assert seq % tq == 0 and tq % LANES == 0
    kern = functools.partial(_prompt_attn_kernel, tq=tq, n_heads=n_heads)
    return pl.pallas_call(
        kern, grid=(n_seq, nq),
        in_specs=[pl.BlockSpec((1, m, 2 * LANES), lambda b, i: (b * nq + i, 0, 0)),
                  pl.BlockSpec((1, m, 2 * LANES), lambda b, i: (b * nq + jnp.minimum(i + 1, nq - 1), 0, 0)),
                  pl.BlockSpec((1, seq, 2 * LANES), lambda b, i: (b, 0, 0))],
        out_specs=pl.BlockSpec((1, tq, n_heads * LANES), lambda b, i: (b, i, 0)),
        out_shape=jax.ShapeDtypeStruct((n_seq, seq, n_heads * LANES), BF16),
        scratch_shapes=[pltpu.VMEM((m, LANES), F32), pltpu.VMEM((m, 2 * LANES), F32), pltpu.VMEM((m, tq), F32)],
        compiler_params=pltpu.CompilerParams(dimension_semantics=("arbitrary", "arbitrary"),
                                             vmem_limit_bytes=VMEM_LIMIT),
        name="prompt_attn",
    )(q, q, kvcat)


def _softmax_update(s, vals, m_sc, l_sc, acc_sc):
    m_prev = m_sc[...]
    m_new = jnp.maximum(m_prev, jnp.max(s, axis=-1, keepdims=True))
    corr = jnp.exp2(m_prev - m_new)
    p = jnp.exp2(s - m_new)
    l_sc[...] = l_sc[...] * corr + jnp.sum(p, axis=-1, keepdims=True)
    acc_sc[...] = acc_sc[...] * corr + _dot(p.astype(BF16), vals)
    m_sc[...] = m_new


SEQS_PER_STEP = 4


def _sample_attn_kernel(pt_ref, q_ref, kvnew_ref, ckv_hbm, kr_hbm, o_ref, ckv_buf, kr_buf, sem, m_sc, l_sc, acc_sc,
                        s_sc, *, n_pages, tile_pages, rope_dim, dec_seq, n_heads, n_seq):
    i = pl.program_id(0)
    tp = tile_pages
    n_tiles = n_pages // tp

    def page_copies(page, slot, p):
        return (pltpu.make_async_copy(ckv_hbm.at[page], ckv_buf.at[slot, p], sem.at[slot, 0]),
                pltpu.make_async_copy(kr_hbm.at[page], kr_buf.at[slot, p], sem.at[slot, 1]))

    def start_gather(seq, slot):
        def body(p, carry):
            for cp in page_copies(pt_ref[seq * n_pages + p], slot, p):
                cp.start()
            return carry
        lax.fori_loop(0, n_pages, body, 0, unroll=8)

    def wait_gather(slot):
        pltpu.make_async_copy(ckv_hbm.at[pl.ds(0, n_pages)], ckv_buf.at[slot], sem.at[slot, 0]).wait()
        pltpu.make_async_copy(kr_hbm.at[pl.ds(0, n_pages)], kr_buf.at[slot], sem.at[slot, 1]).wait()

    def attend(j, slot):
        q = q_ref[j]
        q_lat, q_rope = q[:, :LANES], q[:, LANES:LANES + rope_dim]
        m_sc[...] = jnp.full(m_sc.shape, NEG, F32)
        l_sc[...] = jnp.zeros(l_sc.shape, F32)
        acc_sc[...] = jnp.zeros(acc_sc.shape, F32)

        def vals(t):
            c = ckv_buf[slot, pl.ds(t * tp, tp)]
            return c.reshape(tp * LANES, LANES).astype(BF16)

        def scores(t):
            kr = kr_buf[slot, pl.ds(t * tp, tp)].astype(BF16)
            kr_cat = jnp.concatenate([kr[u] for u in range(tp)], axis=-1)
            return _dot_nt(q_lat, vals(t)) + _dot(q_rope, kr_cat)

        s_sc[...] = scores(0)

        def body(t, carry):
            s = s_sc[...]
            s_next = scores(t + 1)
            _softmax_update(s, vals(t), m_sc, l_sc, acc_sc)
            s_sc[...] = s_next
            return carry

        lax.fori_loop(0, n_tiles - 1, body, 0)
        _softmax_update(s_sc[...], vals(n_tiles - 1), m_sc, l_sc, acc_sc)
        kn = kvnew_ref[j]
        s = _dot_nt(q, kn)
        qi = lax.broadcasted_iota(jnp.int32, s.shape, 0) % dec_seq
        kj = lax.broadcasted_iota(jnp.int32, s.shape, 1)
        _softmax_update(jnp.where(qi >= kj, s, NEG), kn[:, :LANES], m_sc, l_sc, acc_sc)
        o = (acc_sc[...] / l_sc[...]).astype(BF16)
        for h in range(n_heads):
            o_ref[j, :, h * LANES:(h + 1) * LANES] = o[h * dec_seq:(h + 1) * dec_seq]

    first = SEQS_PER_STEP * i

    @pl.when(i == 0)
    def _():
        for j in range(SEQS_PER_STEP):
            start_gather(j, j)

    for j in range(SEQS_PER_STEP):
        wait_gather(j)
        attend(j, j)

        @pl.when(first + SEQS_PER_STEP + j < n_seq)
        def _():
            start_gather(first + SEQS_PER_STEP + j, j)


def sample_attn(q, kvnew, cache_ckv, cache_kr_t, page_table, *, dec_seq, n_heads, tile_pages):
    b, n_pages = page_table.shape
    page = cache_ckv.shape[1]
    rope_dim = cache_kr_t.shape[1]
    assert n_pages % tile_pages == 0 and n_pages % 2 == 0 and page == LANES and b % SEQS_PER_STEP == 0
    m = n_heads * dec_seq
    kern = functools.partial(_sample_attn_kernel, n_pages=n_pages, tile_pages=tile_pages, rope_dim=rope_dim,
                             dec_seq=dec_seq, n_heads=n_heads, n_seq=b)
    per_step = lambda rows, width: pl.BlockSpec((SEQS_PER_STEP, rows, width), lambda i, pt: (i, 0, 0))
    grid_spec = pltpu.PrefetchScalarGridSpec(
        num_scalar_prefetch=1, grid=(b // SEQS_PER_STEP,),
        in_specs=[per_step(m, 2 * LANES), per_step(dec_seq, 2 * LANES),
                  pl.BlockSpec(memory_space=pl.ANY), pl.BlockSpec(memory_space=pl.ANY)],
        out_specs=per_step(dec_seq, n_heads * LANES),
        scratch_shapes=[pltpu.VMEM((SEQS_PER_STEP, n_pages, page, LANES), F32),
                        pltpu.VMEM((SEQS_PER_STEP, n_pages, rope_dim, page), F32),
                        pltpu.SemaphoreType.DMA((SEQS_PER_STEP, 2)),
                        pltpu.VMEM((m, 1), F32), pltpu.VMEM((m, 1), F32), pltpu.VMEM((m, LANES), F32),
                        pltpu.VMEM((m, tile_pages * page), F32)])
    return pl.pallas_call(
        kern, grid_spec=grid_spec,
        out_shape=jax.ShapeDtypeStruct((b, dec_seq, n_heads * LANES), BF16),
        compiler_params=pltpu.CompilerParams(dimension_semantics=("arbitrary",), vmem_limit_bytes=VMEM_LIMIT),
        name="sample_attn",
    )(page_table.reshape(-1), q, kvnew, cache_ckv, cache_kr_t)


def _cumsum_rows(tri, x):
    if x.shape[0] < 16:
        return jnp.dot(tri.astype(F32), x, preferred_element_type=F32, precision=lax.Precision.HIGHEST)
    hi = x.astype(BF16)
    lo = (x - hi.astype(F32)).astype(BF16)
    both = _dot(tri, jnp.concatenate([hi, lo], axis=-1))
    return both[:, :x.shape[1]] + both[:, x.shape[1]:]


def _rwkv_chunk(rs, lws, ks, vs, kks, asigs, s0s, tri, strict, incl, head0):
    n = range(len(rs))
    c = rs[0].shape[0]
    c2 = 2 * c

    def stack(x):
        return jnp.concatenate([jnp.where(head0, x, 0.0), jnp.where(head0, 0.0, x)], axis=0)

    cs = [_cumsum_rows(tri, lws[i]) for i in n]
    cs_end = [cs[i][c - 1:c, :] for i in n]
    e_neg = [jnp.exp(-cs[i]) for i in n]
    e_end = [jnp.exp(cs_end[i] - cs[i]) for i in n]
    bvec = [kks[i] * asigs[i] for i in n]
    a_t = [stack(-kks[i] * jnp.exp(cs[i] - lws[i])) for i in n]
    r_t = [stack(rs[i] * jnp.exp(cs[i])).astype(BF16) for i in n]
    k_t = [stack(ks[i] * e_neg[i]) for i in n]
    b_t = [stack(bvec[i] * e_neg[i]) for i in n]
    k_e = [stack(ks[i] * e_end[i]).astype(BF16) for i in n]
    b_e = [stack(bvec[i] * e_end[i]).astype(BF16) for i in n]
    v_s = [stack(vs[i]).astype(BF16) for i in n]

    sc = [_dot_nt(jnp.concatenate([a_t[i].astype(BF16), r_t[i]], axis=0),
                  jnp.concatenate([k_t[i], b_t[i]], axis=0).astype(BF16)) for i in n]
    l_ak = [jnp.where(strict, sc[i][:c2, :c2], 0.0).astype(BF16) for i in n]
    lp = [jnp.where(strict, sc[i][:c2, c2:], 0.0).astype(BF16) for i in n]
    a_rk = [jnp.where(incl, sc[i][c2:, :c2], 0.0).astype(BF16) for i in n]
    a_rb = [jnp.where(incl, sc[i][c2:, c2:], 0.0).astype(BF16) for i in n]

    x = [jnp.concatenate([a_t[i], _dot(l_ak[i], v_s[i])], axis=-1) for i in n]
    n_fac = int(math.log2(c))
    for f in range(n_fac):
        x = [x[i] + _dot(lp[i], x[i].astype(BF16)) for i in n]
        if f + 1 < n_fac:
            lp = [_dot(lp[i], lp[i]).astype(BF16) for i in n]

    s0b = [s0s[i].astype(BF16) for i in n]
    ar = [_dot_nt(jnp.concatenate([x[i][:, :LANES].astype(BF16), r_t[i]], axis=0), s0b[i]) for i in n]
    u = [(ar[i][:c2] + x[i][:, LANES:]).astype(BF16) for i in n]
    vu = [jnp.concatenate([v_s[i], u[i]], axis=0) for i in n]
    y_st = [ar[i][c2:] + _dot(jnp.concatenate([a_rk[i], a_rb[i]], axis=-1), vu[i]) for i in n]
    ys = [y_st[i][:c] + y_st[i][c:] for i in n]
    s_new = [s0s[i] * jnp.exp(cs_end[i]) + _dot_tn(vu[i], jnp.concatenate([k_e[i], b_e[i]], axis=0)) for i in n]
    return ys, s_new


def _rwkv_kernel(r_ref, lw_ref, k_ref, v_ref, kk_ref, as_ref, g_ref, s0_ref, rk_ref, lg_ref, lb_ref, ones2_ref,
                 y_ref, sout_ref, s_sc, *, chunk, n_pairs, n_seq_blk):
    ci = pl.program_id(1)

    @pl.when(ci == 0)
    def _():
        s_sc[...] = s0_ref[...]

    c = chunk
    ti = lax.broadcasted_iota(jnp.int32, (c, c), 0)
    tj = lax.broadcasted_iota(jnp.int32, (c, c), 1)
    tri = (ti >= tj).astype(BF16)
    si = lax.broadcasted_iota(jnp.int32, (2 * c, 2 * c), 0)
    sj = lax.broadcasted_iota(jnp.int32, (2 * c, 2 * c), 1)
    strict = si > sj
    incl = si >= sj
    head0 = lax.broadcasted_iota(jnp.int32, (c, LANES), 1) < (LANES // HEAD_PAIR)
    ones2 = ones2_ref[...]
    inv_n = 1.0 / (LANES // HEAD_PAIR)

    streams = [(b, p) for b in range(n_seq_blk) for p in range(n_pairs)]
    n = range(len(streams))
    lanes = [slice(p * LANES, (p + 1) * LANES) for _, p in streams]
    get = lambda ref: [ref[b, :, lanes[i]] for i, (b, _) in enumerate(streams)]
    par = lambda ref: [ref[:, lanes[i]] for i in n]
    rs, ks, vs = get(r_ref), get(k_ref), get(v_ref)
    ys, s_new = _rwkv_chunk(rs, get(lw_ref), ks, vs, get(kk_ref), get(as_ref),
                            [s_sc[b, p] for b, p in streams], tri, strict, incl, head0)
    for i, (b, p) in enumerate(streams):
        s_sc[b, p] = s_new[i]
    lg, lb, rk = par(lg_ref), par(lb_ref), par(rk_ref)
    mu = [_head_sums(ys[i], ones2) * inv_n for i in n]
    d = [ys[i] - mu[i] for i in n]
    var = [_head_sums(d[i] * d[i], ones2) * inv_n for i in n]
    bonus = [_head_sums(rs[i] * ks[i] * rk[i], ones2) * vs[i] for i in n]
    gs = get(g_ref)
    for i, (b, p) in enumerate(streams):
        o = d[i] * lax.rsqrt(var[i] + GN_EPS) * lg[i] + lb[i] + bonus[i]
        y_ref[b, :, lanes[i]] = (o * gs[i]).astype(BF16)

    @pl.when(ci == pl.num_programs(1) - 1)
    def _():
        sout_ref[...] = s_sc[...]


def rwkv_chunked(r, lw, k, v, kk, asig, g, s0, rk, lg, lb, ones2, *, chunk, n_seq_blk):
    n_seq, seq, rw_dim = r.shape
    n_pairs = rw_dim // LANES
    nc = seq // chunk
    assert n_seq % n_seq_blk == 0 and seq % chunk == 0
    kern = functools.partial(_rwkv_kernel, chunk=chunk, n_pairs=n_pairs, n_seq_blk=n_seq_blk)
    row = pl.BlockSpec((n_seq_blk, chunk, rw_dim), lambda b, c: (b, c, 0))
    st = pl.BlockSpec((n_seq_blk, n_pairs, LANES, LANES), lambda b, c: (b, 0, 0, 0))
    return pl.pallas_call(
        kern, grid=(n_seq // n_seq_blk, nc),
        in_specs=[row] * 7 + [st] + [_const_spec(x.shape) for x in (rk, lg, lb, ones2)],
        out_specs=(row, st),
        out_shape=(jax.ShapeDtypeStruct((n_seq, seq, rw_dim), BF16), jax.ShapeDtypeStruct(s0.shape, F32)),
        scratch_shapes=[pltpu.VMEM((n_seq_blk, n_pairs, LANES, LANES), F32)],
        compiler_params=pltpu.CompilerParams(dimension_semantics=("arbitrary", "arbitrary"),
                                             vmem_limit_bytes=VMEM_LIMIT),
        name="rwkv_chunked",
    )(r, lw, k, v, kk, asig, g, s0, rk, lg, lb, ones2)


def _block_diag(blocks):
    n, r, c = blocks.shape
    eye = jnp.eye(n, dtype=blocks.dtype)
    return (eye[:, None, :, None] * blocks[:, :, None, :]).reshape(n * r, n * c)


def _rope_group(w, rope_dim):
    half = rope_dim // 2
    x1, x2 = w[..., :half], w[..., half:]
    pad = jnp.zeros(w.shape[:-1] + (LANES - 2 * rope_dim,), w.dtype)
    return jnp.concatenate([x1, x2, -x2, x1, pad], axis=-1)


def _prep_weights(p, dims):
    q_rank, kv_rank, rope_dim, n_heads, nope, rw_dim = (dims[k] for k in
                                                        ("q_rank", "kv_rank", "rope_dim", "n_heads", "nope", "rw_dim"))
    w_in = p["w_in"]
    o1, o2 = q_rank + kv_rank, q_rank + kv_rank + rope_dim
    w_uq = p["w_uq"]
    row = lambda a: a.reshape(1, -1).astype(F32)
    d_wl = p["w_lora_up"].shape[0]
    d_al = p["a_lora_up"].shape[0]
    assert d_wl + d_al == LANES
    half = jnp.arange(0, rope_dim, 2, dtype=F32)
    inv = ROPE_BASE ** (-half / rope_dim)
    invf = jnp.concatenate([inv, inv, jnp.zeros((LANES - rope_dim,), F32)]).reshape(1, LANES)
    rw_head = rw_dim // dims["rw_heads"]
    assert HEAD_PAIR * rw_head == LANES and HEAD_PAIR * nope == LANES and kv_rank == LANES
    pair_ones = _block_diag(jnp.ones((HEAD_PAIR, rw_head, rw_head), F32))
    w_out = p["w_out"]
    d_mla = n_heads * dims["mla_v"]
    wuk = jnp.transpose(p["w_uk"], (1, 2, 0)).reshape(n_heads // HEAD_PAIR, HEAD_PAIR, nope, kv_rank)
    wuv = jnp.transpose(p["w_uv"], (1, 0, 2)).reshape(n_heads // HEAD_PAIR, HEAD_PAIR, kv_rank, dims["mla_v"])
    return {
        "wmla": jnp.concatenate([w_in[:, :o1], _rope_group(w_in[:, o1:o2], rope_dim)], axis=1).astype(BF16),
        "wrw": w_in[:, o2:].astype(BF16),
        "qg": row(p["q_norm_g"]), "kvg": row(p["kv_norm_g"]),
        "wuqn": w_uq[:, :, :nope].reshape(q_rank, n_heads * nope).astype(BF16),
        "wuqr": _rope_group(w_uq[:, :, nope:], rope_dim).reshape(q_rank, n_heads * LANES).astype(BF16),
        "wuk": jnp.stack([_block_diag(w) for w in wuk]).astype(BF16),
        "invf": invf,
        "mu": row(p["shift_mu"]), "w0": row(p["w0"]), "a0": row(p["a0"]),
        "wl": jnp.concatenate([p["w_lora_up"], jnp.zeros((d_al, rw_dim), F32)], axis=0).astype(BF16),
        "al": jnp.concatenate([jnp.zeros((d_wl, rw_dim), F32), p["a_lora_up"]], axis=0).astype(BF16),
        "gl": p["g_lora_up"].astype(BF16),
        "kk": row(p["k_k"]), "ka": row(p["k_a"]),
        "ones2": jnp.concatenate([pair_ones, pair_ones], axis=0).astype(BF16),
        "rk": row(p["r_k"]), "lg": row(p["lnx_g"]), "lb": row(p["lnx_b"]),
        "wuv": jnp.stack([_block_diag(w) for w in wuv]).astype(BF16),
        "woa": w_out[:d_mla].astype(BF16), "wob": w_out[d_mla:].astype(BF16),
    }


def _pair_state(s):
    b, h, n, _ = s.shape
    s = s.reshape(b, h // HEAD_PAIR, HEAD_PAIR, n, n)
    eye = jnp.eye(HEAD_PAIR, dtype=s.dtype)
    return (s[:, :, :, :, None, :] * eye[None, None, :, None, :, None]).reshape(b, h // HEAD_PAIR, HEAD_PAIR * n,
                                                                                 HEAD_PAIR * n)


def _unpair_state(s, n):
    b, hp = s.shape[:2]
    s = s.reshape(b, hp, HEAD_PAIR, n, HEAD_PAIR, n)
    return jnp.stack([s[:, :, i, :, i, :] for i in range(HEAD_PAIR)], axis=2).reshape(b, hp * HEAD_PAIR, n, n)


def _group(x, shift_buf, wkv0, wts, ffa, ffb, lns, dims, *, pos0, attend, tm_in, chunk, rw_seq_blk, alpha):
    n_seq, seq, d = x.shape
    n = n_seq * seq
    n_heads, rw_dim, rope_dim = dims["n_heads"], dims["rw_dim"], dims["rope_dim"]
    x1 = ffn_ln(x.reshape(n, d), *ffa, lns["ln1_g"], lns["ln1_b"], alpha=alpha)
    (q, kvcat, ckv, kr, r, lw, k, v, kk, asig, g, shift_out) = mixer_in(
        x1, shift_buf.reshape(n_seq, 1, -1), wts, n_seq=n_seq, seq=seq, pos0=pos0, tm=tm_in, n_heads=n_heads,
        q_scale=dims["q_scale"], rw_dim=rw_dim, rope_dim=rope_dim)
    ocat = attend(q.reshape(q.shape[0], n_heads * q.shape[2], q.shape[3]), kvcat.reshape(n_seq, seq, -1))
    seqs = lambda a: a.reshape(n_seq, seq, -1)
    yrw, s_out = rwkv_chunked(seqs(r), seqs(lw), seqs(k), seqs(v), seqs(kk), seqs(asig), seqs(g), _pair_state(wkv0),
                              wts["rk"], wts["lg"], wts["lb"], wts["ones2"], chunk=chunk, n_seq_blk=rw_seq_blk)
    y = out_ffn_ln(x1, ocat.reshape(n, -1), yrw.reshape(n, -1), wts["wuv"], wts["woa"], wts["wob"], lns["ln2_g"], lns["ln2_b"],
                   *ffb, lns["ln3_g"], lns["ln3_b"], alpha=alpha)
    return (y.reshape(n_seq, seq, d), ckv.reshape(1, n_seq, seq, -1), kr.reshape(1, n_seq, seq, -1),
            _unpair_state(s_out, rw_dim // dims["rw_heads"])[None], shift_out.reshape(1, n_seq, -1))


def kernel(x_prompt, x_sample, cache_ckv, cache_krope, state_wkv, state_shift, page_table, ln1_g, ln1_b, ffa_w_gate, ffa_w_up, ffa_w_down, w_in, q_norm_g, w_uq, kv_norm_g, w_uk, w_uv, shift_mu, w0, w_lora_up, a0, a_lora_up, g_lora_up, k_k, k_a, r_k, lnx_g, lnx_b, w_out, ln2_g, ln2_b, ffb_w_gate, ffb_w_up, ffb_w_down, ln3_g, ln3_b):
    depth = w_in.shape[0]
    assert depth == 1
    alpha = (2 * depth) ** 0.25
    n_heads, nope_rope = w_uq.shape[2], w_uq.shape[3]
    rope_dim = cache_krope.shape[-1]
    rw_heads, rw_head = state_wkv.shape[2], state_wkv.shape[3]
    dims = dict(q_rank=w_uq.shape[1], kv_rank=w_uk.shape[1], rope_dim=rope_dim, n_heads=n_heads,
                nope=nope_rope - rope_dim, rw_dim=rw_heads * rw_head, rw_heads=rw_heads, mla_v=w_uv.shape[3],
                q_scale=float(nope_rope) ** -0.5 * math.log2(math.e))
    p = dict(w_in=w_in[0], q_norm_g=q_norm_g[0], w_uq=w_uq[0], kv_norm_g=kv_norm_g[0], w_uk=w_uk[0], w_uv=w_uv[0],
             shift_mu=shift_mu[0], w0=w0[0], w_lora_up=w_lora_up[0], a0=a0[0], a_lora_up=a_lora_up[0],
             g_lora_up=g_lora_up[0], k_k=k_k[0], k_a=k_a[0], r_k=r_k[0], lnx_g=lnx_g[0], lnx_b=lnx_b[0],
             w_out=w_out[0])
    wts = _prep_weights(p, dims)
    row = lambda a: a[0].reshape(1, -1)
    lns = dict(ln1_g=row(ln1_g), ln1_b=row(ln1_b), ln2_g=row(ln2_g), ln2_b=row(ln2_b), ln3_g=row(ln3_g),
               ln3_b=row(ln3_b))
    ffa = (ffa_w_gate[0].astype(BF16), ffa_w_up[0].astype(BF16), ffa_w_down[0].astype(BF16))
    ffb = (ffb_w_gate[0].astype(BF16), ffb_w_up[0].astype(BF16), ffb_w_down[0].astype(BF16))

    bp, sp, _ = x_prompt.shape
    bs, ss, _ = x_sample.shape
    n_pages, page = page_table.shape[1], cache_ckv.shape[2]
    past_len = n_pages * page
    tq = min(512, sp)

    attend_p = functools.partial(prompt_attn, n_seq=bp, seq=sp, tq=tq, n_heads=n_heads)
    out_p = _group(x_prompt, jnp.zeros((bp, state_shift.shape[-1]), F32),
                   jnp.zeros((bp,) + state_wkv.shape[2:], F32), wts, ffa, ffb, lns, dims,
                   pos0=0.0, attend=attend_p, tm_in=tq, chunk=min(64, sp), rw_seq_blk=math.gcd(bp, 4), alpha=alpha)

    kr_t = jnp.swapaxes(cache_krope[0], 1, 2)
    attend_s = lambda q, kvnew: sample_attn(q, kvnew, cache_ckv[0], kr_t, page_table, dec_seq=ss,
                                            n_heads=n_heads, tile_pages=math.gcd(n_pages, 32))
    out_s = _group(x_sample, state_shift[0], state_wkv[0], wts, ffa, ffb, lns, dims,
                   pos0=float(past_len), attend=attend_s, tm_in=min(128, bs * ss), chunk=ss,
                   rw_seq_blk=math.gcd(bs, 8), alpha=alpha)
    return (out_p[0], out_s[0]) + out_p[1:] + out_s[1:]
```
